```python
import math
import jax, jax.numpy as jnp
from jax import lax
import numpy as np


D_MODEL = 1024
BATCH = 2
SEQ = 8192
DEPTH = 4
DEC_BATCH = 128
DEC_SEQ = 1
PAST_LEN = 2048
PAGE_SIZE = 128

HEAD_DIM = 128
MB_HEADS = D_MODEL // HEAD_DIM
GDN_HEADS = D_MODEL // HEAD_DIM
MB_W = MB_HEADS * HEAD_DIM
GDN_W = GDN_HEADS * HEAD_DIM
MOBA_BLOCK = 256
MOBA_TOPK = 3
MOBA_QCHUNK = 32
GDN_CHUNK = 64
CONV_W = 4
D_FF = -(-8 * D_MODEL // (3 * 256)) * 256
IN_COLS = 3 * MB_W + 3 * GDN_W + GDN_W + 2 * GDN_HEADS + 2 * D_MODEL
ROPE_THETA = 10000.0
EPS = 1e-6

kernel_name = "gated_deltanet_moba_parallel_decoder_step"


def rms_norm(x, w):
    xf = x.astype(jnp.float32)
    y = xf * lax.rsqrt(jnp.mean(xf * xf, axis=-1, keepdims=True) + EPS)
    return (y * w.astype(jnp.float32)).astype(x.dtype)


def l2norm(x):
    return x * lax.rsqrt(jnp.sum(x * x, axis=-1, keepdims=True) + 1e-6)


def rope(x, pos):
    half = x.shape[-1] // 2
    inv = jnp.exp(-math.log(ROPE_THETA) * jnp.arange(half, dtype=jnp.float32) / half)
    ang = pos.astype(jnp.float32)[:, None] * inv[None, :]
    cos = jnp.cos(ang)[None, :, None, :]
    sin = jnp.sin(ang)[None, :, None, :]
    xf = x.astype(jnp.float32)
    x1, x2 = xf[..., :half], xf[..., half:]
    return jnp.concatenate([x1 * cos - x2 * sin, x2 * cos + x1 * sin], axis=-1).astype(x.dtype)


def split_cols(p):
    sizes = (MB_W, MB_W, MB_W, 3 * GDN_W, GDN_W, GDN_HEADS, GDN_HEADS, D_MODEL, D_MODEL)
    idx, acc = [], 0
    for s in sizes[:-1]:
        acc += s
        idx.append(acc)
    return jnp.split(p, idx, axis=-1)


def in_proj(x, ln_w, w_in, pos):
    B, T, _ = x.shape
    p = rms_norm(x, ln_w) @ w_in
    mq, mk, mv, qkv, z, b, a, g_gdn, g_moba = split_cols(p)
    mq = rope(mq.reshape(B, T, MB_HEADS, HEAD_DIM), pos)
    mk = rope(mk.reshape(B, T, MB_HEADS, HEAD_DIM), pos)
    mv = mv.reshape(B, T, MB_HEADS, HEAD_DIM)
    return mq, mk, mv, qkv, z, b, a, g_gdn, g_moba


def moba_blocks(k, v):
    B, L, H, D = k.shape
    nb = -(-L // MOBA_BLOCK)
    pad = nb * MOBA_BLOCK - L
    kp = jnp.pad(k, ((0, 0), (0, pad), (0, 0), (0, 0)))
    vp = jnp.pad(v, ((0, 0), (0, pad), (0, 0), (0, 0)))
    kb = kp.reshape(B, nb, MOBA_BLOCK, H, D).transpose(0, 3, 1, 2, 4)
    vb = vp.reshape(B, nb, MOBA_BLOCK, H, D).transpose(0, 3, 1, 2, 4)
    kmean = jnp.mean(kb.astype(jnp.float32), axis=3)
    return kb, vb, kmean


def moba_attend(q, q_pos, kb, vb, kmean):
    B, Q, H, D = q.shape
    nb = kb.shape[2]
    n_top = min(MOBA_TOPK, nb)
    q_blk = q_pos // MOBA_BLOCK
    gate = jnp.einsum('bqhd,bhnd->bqhn', q.astype(jnp.float32), kmean)
    fully_past = jnp.arange(nb)[None, :] < q_blk[:, None]
    gate = jnp.where(fully_past[None, :, None, :], gate, -jnp.inf)
    _, top = lax.top_k(gate, n_top)
    own = jnp.broadcast_to(q_blk[None, :, None, None], (B, Q, H, 1)).astype(top.dtype)
    sel = jnp.concatenate([top, own], axis=-1)
    sel_ok = jnp.concatenate([jnp.arange(n_top)[None, :] < q_blk[:, None],
                              jnp.ones((Q, 1), dtype=bool)], axis=-1)
    bi = jnp.arange(B)[:, None, None, None]
    hi = jnp.arange(H)[None, None, :, None]
    kg = kb[bi, hi, sel]
    vg = vb[bi, hi, sel]
    k_pos = sel[..., None] * MOBA_BLOCK + jnp.arange(MOBA_BLOCK)
    mask = sel_ok[None, :, None, :, None] & (k_pos <= q_pos[None, :, None, None, None])
    s = jnp.einsum('bqhd,bqhnkd->bqhnk', q, kg).astype(jnp.float32) * (HEAD_DIM ** -0.5)
    s = jnp.where(mask, s, -jnp.inf).reshape(B, Q, H, -1)
    p = jax.nn.softmax(s, axis=-1).reshape(mask.shape)
    return jnp.einsum('bqhnk,bqhnkd->bqhd', p.astype(vg.dtype), vg)


def moba_prompt(q, pos, kb, vb, kmean):
    B, T, H, D = q.shape
    nq = T // MOBA_QCHUNK
    qc = jnp.moveaxis(q.reshape(B, nq, MOBA_QCHUNK, H, D), 1, 0)
    pc = pos.reshape(nq, MOBA_QCHUNK)
    o = lax.map(lambda a: moba_attend(a[0], a[1], kb, vb, kmean), (qc, pc))
    return jnp.moveaxis(o, 0, 1).reshape(B, T, H, D)


def short_conv(u, prev, w):
    T = u.shape[1]
    up = jnp.concatenate([prev.astype(u.dtype), u], axis=1)
    y = up[:, 0:T] * w[0]
    for i in range(1, CONV_W):
        y = y + up[:, i:i + T] * w[i]
    return jax.nn.silu(y), up[:, -(CONV_W - 1):]


def gdn_prep(qkv, b, a, conv_prev, conv_w, a_log, dt_bias):
    B, T, _ = qkv.shape
    c, conv_new = short_conv(qkv, conv_prev, conv_w)
    c = c.astype(jnp.float32)
    q, k, v = jnp.split(c, 3, axis=-1)
    q = l2norm(q.reshape(B, T, GDN_HEADS, HEAD_DIM)) * (HEAD_DIM ** -0.5)
    k = l2norm(k.reshape(B, T, GDN_HEADS, HEAD_DIM))
    v = v.reshape(B, T, GDN_HEADS, HEAD_DIM)
    beta = jax.nn.sigmoid(b.astype(jnp.float32))
    g = -jnp.exp(a_log.astype(jnp.float32)) * jax.nn.softplus(
        a.astype(jnp.float32) + dt_bias.astype(jnp.float32))
    return q, k, v, beta, g, conv_new


def gdn_chunked(q, k, v, beta, g, s0):
    B, T, H, DK = q.shape
    DV = v.shape[-1]
    C = GDN_CHUNK
    N = T // C

    def blocks(t):
        return jnp.moveaxis(t.reshape((B, N, C) + t.shape[2:]), 3, 1)

    q, k, v, beta, g = blocks(q), blocks(k), blocks(v), blocks(beta), blocks(g)
    gc = jnp.cumsum(g, axis=-1)
    diff = gc[..., :, None] - gc[..., None, :]
    idx = jnp.arange(C)
    incl = idx[:, None] >= idx[None, :]
    strict = idx[:, None] > idx[None, :]
    dec_incl = jnp.exp(jnp.where(incl, diff, -jnp.inf))
    dec_strict = jnp.where(strict, dec_incl, 0.0)
    kk = jnp.einsum('bhncd,bhnjd->bhncj', k, k)
    a_mat = jnp.eye(C, dtype=jnp.float32) + beta[..., :, None] * kk * dec_strict
    rhs = jnp.concatenate([beta[..., None] * v, (beta * jnp.exp(gc))[..., None] * k], axis=-1)
    sol = lax.linalg.triangular_solve(a_mat, rhs, left_side=True, lower=True, unit_diagonal=True)
    u_v, w = sol[..., :DV], sol[..., DV:]
    qk = jnp.einsum('bhncd,bhnjd->bhncj', q, k) * dec_incl
    q_dec = q * jnp.exp(gc)[..., None]
    k_dec = k * jnp.exp(gc[..., -1:] - gc)[..., None]
    chunk_decay = jnp.exp(gc[..., -1])
    xs = (jnp.moveaxis(u_v, 2, 0), jnp.moveaxis(w, 2, 0), jnp.moveaxis(qk, 2, 0),
          jnp.moveaxis(q_dec, 2, 0), jnp.moveaxis(k_dec, 2, 0), jnp.moveaxis(chunk_decay, 2, 0))

    def step(s, inp):
        u_v_n, w_n, qk_n, q_dec_n, k_dec_n, cd_n = inp
        u = u_v_n - jnp.einsum('bhck,bhkv->bhcv', w_n, s)
        o = jnp.einsum('bhck,bhkv->bhcv', q_dec_n, s) + jnp.einsum('bhcj,bhjv->bhcv', qk_n, u)
        s = cd_n[..., None, None] * s + jnp.einsum('bhck,bhcv->bhkv', k_dec_n, u)
        return s, o

    s, o = lax.scan(step, s0, xs)
    o = jnp.moveaxis(jnp.moveaxis(o, 0, 2), 1, 3).reshape(B, T, H, DV)
    return o, s


def gdn_recurrent(q, k, v, beta, g, s0):
    xs = (jnp.moveaxis(q, 1, 0), jnp.moveaxis(k, 1, 0), jnp.moveaxis(v, 1, 0),
          jnp.moveaxis(beta, 1, 0), jnp.moveaxis(g, 1, 0))

    def step(s, inp):
        q_t, k_t, v_t, b_t, g_t = inp
        s = jnp.exp(g_t)[..., None, None] * s
        u = b_t[..., None] * (v_t - jnp.einsum('bhk,bhkv->bhv', k_t, s))
        s = s + k_t[..., :, None] * u[..., None, :]
        o = jnp.einsum('bhk,bhkv->bhv', q_t, s)
        return s, o

    s, o = lax.scan(step, s0, xs)
    return jnp.moveaxis(o, 0, 1), s


def gdn_out(o, z, w):
    B, T, H, DV = o.shape
    on = o * lax.rsqrt(jnp.mean(o * o, axis=-1, keepdims=True) + EPS) * w.astype(jnp.float32)
    zf = z.reshape(B, T, H, DV).astype(jnp.float32)
    return (on * jax.nn.silu(zf)).reshape(B, T, H * DV).astype(z.dtype)


def merge_out(x, y_gdn, y_moba, g_gdn, g_moba, w_pg, w_pm, w_o):
    B, T, _ = x.shape
    yg = y_gdn @ w_pg
    ym = y_moba.reshape(B, T, MB_W) @ w_pm
    mixed = jax.nn.sigmoid(g_gdn) * yg + jax.nn.sigmoid(g_moba) * ym
    return x + mixed @ w_o


def ffn(x, ln_w, w_gu, w_dn):
    gu = rms_norm(x, ln_w) @ w_gu
    g, u = jnp.split(gu, 2, axis=-1)
    return x + (jax.nn.silu(g) * u) @ w_dn


def setup_inputs(seed: int = 0) -> dict:
    key = jax.random.key(seed)
    ks = jax.random.split(key, 24)
    f32 = jnp.float32
    n_pages = PAST_LEN // PAGE_SIZE
    n_used = DEC_BATCH * n_pages
    n_pool = n_used + n_used // 4
    perm = jax.random.permutation(ks[0], n_pool).astype(jnp.int32)
    page_table = perm[:n_used].reshape(DEC_BATCH, n_pages)
    x_prompt = jax.random.normal(ks[1], (BATCH, SEQ, D_MODEL), f32)
    x_sample = jax.random.normal(ks[2], (DEC_BATCH, DEC_SEQ, D_MODEL), f32)
    cache_k = jax.random.normal(ks[3], (DEPTH, n_pool, PAGE_SIZE, MB_HEADS, HEAD_DIM), f32)
    cache_v = jax.random.normal(ks[4], (DEPTH, n_pool, PAGE_SIZE, MB_HEADS, HEAD_DIM), f32)
    state_ssm = 0.1 * jax.random.normal(ks[5], (DEPTH, DEC_BATCH, GDN_HEADS, HEAD_DIM, HEAD_DIM), f32)
    state_conv = jax.random.normal(ks[6], (DEPTH, DEC_BATCH, CONV_W - 1, 3 * GDN_W), f32)
    ln_mix = 1.0 + 0.02 * jax.random.normal(ks[7], (DEPTH, D_MODEL), f32)
    ln_ffn = 1.0 + 0.02 * jax.random.normal(ks[8], (DEPTH, D_MODEL), f32)
    w_in = jax.random.normal(ks[9], (DEPTH, D_MODEL, IN_COLS), f32) * D_MODEL ** -0.5
    conv_w = jax.random.normal(ks[10], (DEPTH, CONV_W, 3 * GDN_W), f32) * CONV_W ** -0.5
    a_log = jnp.log(jax.random.uniform(ks[11], (DEPTH, GDN_HEADS), f32, 1.0, 16.0))
    dt = jnp.exp(jax.random.uniform(ks[12], (DEPTH, GDN_HEADS), f32, math.log(1e-3), math.log(0.1)))
    dt_bias = dt + jnp.log(-jnp.expm1(-dt))
    gdn_norm = 1.0 + 0.02 * jax.random.normal(ks[13], (DEPTH, HEAD_DIM), f32)
    w_branch_gdn = jax.random.normal(ks[14], (DEPTH, GDN_W, D_MODEL), f32) * GDN_W ** -0.5
    w_branch_moba = jax.random.normal(ks[15], (DEPTH, MB_W, D_MODEL), f32) * MB_W ** -0.5
    w_out = jax.random.normal(ks[16], (DEPTH, D_MODEL, D_MODEL), f32) * D_MODEL ** -0.5
    w_gate_up = jax.random.normal(ks[17], (DEPTH, D_MODEL, 2 * D_FF), f32) * D_MODEL ** -0.5
    w_down = jax.random.normal(ks[18], (DEPTH, D_FF, D_MODEL), f32) * D_FF ** -0.5
    ln_final = 1.0 + 0.02 * jax.random.normal(ks[19], (D_MODEL,), f32)
    return {"x_prompt": x_prompt, "x_sample": x_sample, "cache_k": cache_k, "cache_v": cache_v,
            "state_ssm": state_ssm, "state_conv": state_conv, "page_table": page_table,
            "ln_mix": ln_mix, "ln_ffn": ln_ffn, "w_in": w_in, "conv_w": conv_w, "a_log": a_log,
            "dt_bias": dt_bias, "gdn_norm": gdn_norm, "w_branch_gdn": w_branch_gdn,
            "w_branch_moba": w_branch_moba, "w_out": w_out, "w_gate_up": w_gate_up,
            "w_down": w_down, "ln_final": ln_final}


def reference(x_prompt, x_sample, cache_k, cache_v, state_ssm, state_conv, page_table,
              ln_mix, ln_ffn, w_in, conv_w, a_log, dt_bias, gdn_norm, w_branch_gdn,
              w_branch_moba, w_out, w_gate_up, w_down, ln_final):
    B, T, _ = x_prompt.shape
    DB, TS, _ = x_sample.shape
    n_pages = page_table.shape[1]
    past = n_pages * PAGE_SIZE
    pos_p = jnp.arange(T, dtype=jnp.int32)
    pos_s = past + jnp.arange(TS, dtype=jnp.int32)
    xp, xs = x_prompt, x_sample
    kp_l, vp_l, sp_l, cp_l, ks_l, vs_l, ss_l, cs_l = [], [], [], [], [], [], [], []
    for l in range(DEPTH):
        mq, mk, mv, qkv, z, b, a, g_gdn, g_moba = in_proj(xp, ln_mix[l], w_in[l], pos_p)
        kb, vb, km = moba_blocks(mk, mv)
        ym = moba_prompt(mq, pos_p, kb, vb, km)
        conv0 = jnp.zeros((B, CONV_W - 1, 3 * GDN_W), qkv.dtype)
        q, k, v, beta, g, conv_new = gdn_prep(qkv, b, a, conv0, conv_w[l], a_log[l], dt_bias[l])
        s0 = jnp.zeros((B, GDN_HEADS, HEAD_DIM, HEAD_DIM), jnp.float32)
        o, s_fin = gdn_chunked(q, k, v, beta, g, s0)
        yg = gdn_out(o, z, gdn_norm[l])
        xp = merge_out(xp, yg, ym, g_gdn, g_moba, w_branch_gdn[l], w_branch_moba[l], w_out[l])
        xp = ffn(xp, ln_ffn[l], w_gate_up[l], w_down[l])
        kp_l.append(mk.reshape(B, T // PAGE_SIZE, PAGE_SIZE, MB_HEADS, HEAD_DIM))
        vp_l.append(mv.reshape(B, T // PAGE_SIZE, PAGE_SIZE, MB_HEADS, HEAD_DIM))
        sp_l.append(s_fin.astype(state_ssm.dtype))
        cp_l.append(conv_new.astype(state_conv.dtype))
        mq, mk, mv, qkv, z, b, a, g_gdn, g_moba = in_proj(xs, ln_mix[l], w_in[l], pos_s)
        past_k = cache_k[l][page_table].reshape(DB, past, MB_HEADS, HEAD_DIM)
        past_v = cache_v[l][page_table].reshape(DB, past, MB_HEADS, HEAD_DIM)
        kb, vb, km = moba_blocks(jnp.concatenate([past_k.astype(mk.dtype), mk], axis=1),
                                 jnp.concatenate([past_v.astype(mv.dtype), mv], axis=1))
        ym = moba_attend(mq, pos_s, kb, vb, km)
        q, k, v, beta, g, conv_new = gdn_prep(qkv, b, a, state_conv[l], conv_w[l], a_log[l], dt_bias[l])
        o, s_fin = gdn_recurrent(q, k, v, beta, g, state_ssm[l].astype(jnp.float32))
        yg = gdn_out(o, z, gdn_norm[l])
        xs = merge_out(xs, yg, ym, g_gdn, g_moba, w_branch_gdn[l], w_branch_moba[l], w_out[l])
        xs = ffn(xs, ln_ffn[l], w_gate_up[l], w_down[l])
        ks_l.append(mk)
        vs_l.append(mv)
        ss_l.append(s_fin.astype(state_ssm.dtype))
        cs_l.append(conv_new.astype(state_conv.dtype))
    y_prompt = rms_norm(xp, ln_final)
    y_sample = rms_norm(xs, ln_final)
    k_prompt = jnp.stack(kp_l)
    v_prompt = jnp.stack(vp_l)
    ssm_prompt = jnp.stack(sp_l)
    conv_prompt = jnp.stack(cp_l)
    k_sample = jnp.stack(ks_l)
    v_sample = jnp.stack(vs_l)
    ssm_sample = jnp.stack(ss_l)
    conv_sample = jnp.stack(cs_l)
    return (y_prompt, y_sample, k_prompt, v_prompt, ssm_prompt, conv_prompt,
            k_sample, v_sample, ssm_sample, conv_sample)
```

```python
import functools
import math

import jax
import jax.numpy as jnp
from jax import lax
from jax.experimental import pallas as pl
from jax.experimental.pallas import tpu as pltpu

F32 = jnp.float32
BF16 = jnp.bfloat16

HEAD_DIM = 128
MOBA_BLOCK = 256
MOBA_TOPK = 3
GDN_CHUNK = 64
CONV_W = 4
PAGE_SIZE = 128
ROPE_THETA = 10000.0
EPS = 1e-6
SUBLANES = 8
VMEM_LIMIT_BYTES = 50 * 1024 * 1024

NT_DIMS = (((1,), (1,)), ((), ()))
TN_DIMS = (((0,), (0,)), ((), ()))


def _params(*semantics):
    return pltpu.CompilerParams(dimension_semantics=semantics, vmem_limit_bytes=VMEM_LIMIT_BYTES)


def _mm(a, b, dims=None):
    a = a.astype(BF16)
    b = b.astype(BF16)
    if dims is None:
        return jnp.dot(a, b, preferred_element_type=F32)
    return lax.dot_general(a, b, dims, preferred_element_type=F32)


def _split3(x):
    hi = x.astype(BF16)
    r1 = x - hi.astype(F32)
    mid = r1.astype(BF16)
    lo = (r1 - mid.astype(F32)).astype(BF16)
    return hi, mid, lo


def _mm_exact_rhs(a, b_exact, dims=None):
    out = None
    for part in _split3(a):
        t = _mm(part, b_exact, dims)
        out = t if out is None else out + t
    return out


def _mm_exact_lhs(a_exact, b, dims=None):
    out = None
    for part in _split3(b):
        t = _mm(a_exact, part, dims)
        out = t if out is None else out + t
    return out


def _mm3(a, b):
    a_hi = a.astype(BF16)
    a_lo = (a - a_hi.astype(F32)).astype(BF16)
    b_hi = b.astype(BF16)
    b_lo = (b - b_hi.astype(F32)).astype(BF16)
    return _mm(a_hi, b_hi) + (_mm(a_hi, b_lo) + _mm(a_lo, b_hi))


def _sigmoid(x):
    return 1.0 / (1.0 + jnp.exp(-x))


def _silu(x):
    return x * _sigmoid(x)


def _softplus(x):
    return jnp.maximum(x, 0.0) + jnp.log1p(jnp.exp(-jnp.abs(x)))


def _in_proj_kernel(x_ref, ln_ref, w_ref, wba_ref, wbat_ref, cos_ref, sin_ref,
                    p_ref, ba_ref, bat_ref, xn_ref, *, n_rope_tiles, first_sigmoid_tile):
    j = pl.program_id(1)

    @pl.when(j == 0)
    def _():
        x = x_ref[...]
        y = x * lax.rsqrt(jnp.mean(x * x, axis=-1, keepdims=True) + EPS)
        xn = (y * ln_ref[...]).astype(BF16)
        xn_ref[...] = xn
        ba_ref[...] = jnp.dot(xn, wba_ref[...], preferred_element_type=F32)
        bat_ref[...] = lax.dot_general(wbat_ref[...], xn, NT_DIMS, preferred_element_type=F32)

    acc = jnp.dot(xn_ref[...], w_ref[...], preferred_element_type=F32)

    @pl.when(j < n_rope_tiles)
    def _():
        cos = cos_ref[...]
        sin = sin_ref[...]
        for h in range(acc.shape[1] // HEAD_DIM):
            sl = slice(h * HEAD_DIM, (h + 1) * HEAD_DIM)
            xh = acc[:, sl]
            p_ref[:, sl] = xh * cos + pltpu.roll(xh, HEAD_DIM // 2, 1) * sin

    @pl.when((j >= n_rope_tiles) & (j < first_sigmoid_tile))
    def _():
        p_ref[...] = acc

    @pl.when(j >= first_sigmoid_tile)
    def _():
        p_ref[...] = _sigmoid(acc)


def _in_proj(x, ln, w_main, w_ba, w_bat, cos, sin, *, tm, tn, n_rope_tiles, first_sigmoid_tile):
    m, d = x.shape
    n = w_main.shape[1]
    rope_blocks = cos.shape[0] // tm
    kern = functools.partial(_in_proj_kernel, n_rope_tiles=n_rope_tiles,
                             first_sigmoid_tile=first_sigmoid_tile)
    return pl.pallas_call(
        kern,
        grid=(m // tm, n // tn),
        in_specs=[
            pl.BlockSpec((tm, d), lambda i, j: (i, 0)),
            pl.BlockSpec((1, d), lambda i, j: (0, 0)),
            pl.BlockSpec((d, tn), lambda i, j: (0, j)),
            pl.BlockSpec((d, HEAD_DIM), lambda i, j: (0, 0)),
            pl.BlockSpec((2 * SUBLANES, d), lambda i, j: (0, 0)),
            pl.BlockSpec((tm, HEAD_DIM), lambda i, j: (i % rope_blocks, 0)),
            pl.BlockSpec((tm, HEAD_DIM), lambda i, j: (i % rope_blocks, 0)),
        ],
        out_specs=[
            pl.BlockSpec((tm, tn), lambda i, j: (i, j)),
            pl.BlockSpec((tm, HEAD_DIM), lambda i, j: (i, 0)),
            pl.BlockSpec((2 * SUBLANES, tm), lambda i, j: (0, i)),
        ],
        out_shape=[
            jax.ShapeDtypeStruct((m, n), F32),
            jax.ShapeDtypeStruct((m, HEAD_DIM), F32),
            jax.ShapeDtypeStruct((2 * SUBLANES, m), F32),
        ],
        scratch_shapes=[pltpu.VMEM((tm, d), BF16)],
        compiler_params=_params("parallel", "arbitrary"),
        name="in_proj",
    )(x, ln, w_main, w_ba, w_bat, cos, sin)


def _top_blocks(gate, n_valid, n_top):
    rows, n = gate.shape
    lane = lax.broadcasted_iota(jnp.int32, (rows, n), 1).astype(F32)
    g = jnp.where(lane < n_valid, gate, -jnp.inf)
    picks = []
    for _ in range(n_top):
        m = jnp.max(g, axis=1, keepdims=True)
        cand = jnp.where((g == m) & (m > -jnp.inf), lane, float(n))
        idx = jnp.min(cand, axis=1, keepdims=True)
        picks.append(idx)
        g = jnp.where(lane == idx, -jnp.inf, g)
    return picks


def _moba_prompt_kernel(q_ref, k_ref, v_ref, o_ref, kb_ref, vb_ref, kmean_ref, *, n_blocks):
    qb = pl.program_id(2)
    blk = MOBA_BLOCK
    scale = HEAD_DIM ** -0.5

    @pl.when(qb == 0)
    def _():
        def body(j, carry):
            rows = pl.ds(pl.multiple_of(j * blk, blk), blk)
            kj = k_ref[rows, :]
            kb_ref[rows, :] = kj.astype(BF16)
            vb_ref[rows, :] = v_ref[rows, :].astype(BF16)
            kmean_ref[pl.ds(j, 1), :] = jnp.mean(kj, axis=0, keepdims=True)
            return carry
        lax.fori_loop(0, n_blocks, body, 0)

    q = q_ref[...]
    gate = lax.dot_general(q, kmean_ref[...], NT_DIMS, preferred_element_type=F32,
                           precision=lax.Precision.HIGHEST)
    picks = _top_blocks(gate, qb.astype(F32), min(MOBA_TOPK, n_blocks))
    q16 = q.astype(BF16)

    def attend(j, mask, carry):
        m, l, acc = carry
        rows = pl.ds(pl.multiple_of(j * blk, blk), blk)
        s = lax.dot_general(q16, kb_ref[rows, :], NT_DIMS, preferred_element_type=F32) * scale
        s = jnp.where(mask, s, -jnp.inf)
        m_new = jnp.maximum(m, jnp.max(s, axis=1, keepdims=True))
        alpha = jnp.exp(m - m_new)
        p = jnp.exp(s - m_new)
        l = alpha * l + jnp.sum(p, axis=1, keepdims=True)
        acc = alpha * acc + jnp.dot(p.astype(BF16), vb_ref[rows, :], preferred_element_type=F32)
        return m_new, l, acc

    row = lax.broadcasted_iota(jnp.int32, (blk, blk), 0)
    col = lax.broadcasted_iota(jnp.int32, (blk, blk), 1)
    init = (jnp.full((blk, 1), -jnp.inf, F32), jnp.zeros((blk, 1), F32), jnp.zeros((blk, HEAD_DIM), F32))
    carry = attend(qb, col <= row, init)

    def past(j, carry):
        jf = j.astype(F32)
        sel = (picks[0] == jf)
        for t in range(1, len(picks)):
            sel = sel | (picks[t] == jf)
        return attend(j, jnp.broadcast_to(sel, (blk, blk)), carry)

    m, l, acc = lax.fori_loop(0, qb, past, carry)
    o_ref[...] = acc / l


def _moba_prompt(p3, *, q_col, k_col, v_col):
    b, t, _ = p3.shape
    n_heads = 8
    nq = t // MOBA_BLOCK
    kern = functools.partial(_moba_prompt_kernel, n_blocks=nq)
    return pl.pallas_call(
        kern,
        grid=(b, n_heads, nq),
        in_specs=[
            pl.BlockSpec((None, MOBA_BLOCK, HEAD_DIM), lambda bi, h, qi: (bi, qi, q_col + h)),
            pl.BlockSpec((None, t, HEAD_DIM), lambda bi, h, qi: (bi, 0, k_col + h)),
            pl.BlockSpec((None, t, HEAD_DIM), lambda bi, h, qi: (bi, 0, v_col + h)),
        ],
        out_specs=pl.BlockSpec((None, MOBA_BLOCK, HEAD_DIM), lambda bi, h, qi: (bi, qi, h)),
        out_shape=jax.ShapeDtypeStruct((b, t, n_heads * HEAD_DIM), F32),
        scratch_shapes=[
            pltpu.VMEM((t, HEAD_DIM), BF16),
            pltpu.VMEM((t, HEAD_DIM), BF16),
            pltpu.VMEM((nq, HEAD_DIM), F32),
        ],
        compiler_params=_params("parallel", "parallel", "arbitrary"),
        name="moba_prompt",
    )(p3, p3, p3)


def _moba_sample_kernel(pt_ref, q_ref, kn_ref, vn_ref, kc_ref, vc_ref, o_ref,
                        qbd_ref, m_ref, l_ref, acc_ref, ksum_ref, *, n_pages, n_heads):
    del pt_ref
    pg = pl.program_id(1)
    width = n_heads * HEAD_DIM
    scale = HEAD_DIM ** -0.5
    head_of_lane = lax.broadcasted_iota(jnp.int32, (n_heads, width), 1) // HEAD_DIM
    diag = head_of_lane == lax.broadcasted_iota(jnp.int32, (n_heads, width), 0)

    @pl.when(pg == 0)
    def _():
        qbd_ref[...] = jnp.where(diag, jnp.broadcast_to(q_ref[...], (n_heads, width)), 0.0)

    qbd = qbd_ref[...]
    k = kc_ref[...]
    s = lax.dot_general(qbd.astype(BF16), k.astype(BF16), NT_DIMS, preferred_element_type=F32) * scale
    m = jnp.max(s, axis=1, keepdims=True)
    p = jnp.exp(s - m)
    m_ref[pg] = m
    l_ref[pg] = jnp.sum(p, axis=1, keepdims=True)
    pv = jnp.dot(p.astype(BF16), vc_ref[...].astype(BF16), preferred_element_type=F32)
    acc_ref[pg] = jnp.where(diag, pv, 0.0)
    ksum_ref[pl.ds(pg, 1), :] = jnp.sum(k, axis=0, keepdims=True)

    @pl.when(pg == n_pages - 1)
    def _():
        pages_per_block = MOBA_BLOCK // PAGE_SIZE
        n_past_blocks = n_pages // pages_per_block
        gates = []
        for j in range(n_past_blocks):
            ks = ksum_ref[pl.ds(j * pages_per_block, 1), :]
            for t in range(1, pages_per_block):
                ks = ks + ksum_ref[pl.ds(j * pages_per_block + t, 1), :]
            kmean = ks * (1.0 / MOBA_BLOCK)
            gates.append(jnp.sum(qbd * kmean, axis=1, keepdims=True))
        gate = jnp.concatenate(gates, axis=1)
        n_total_blocks = n_past_blocks + 1
        picks = _top_blocks(gate, float(n_past_blocks), min(MOBA_TOPK, n_total_blocks))

        def page_selected(page):
            jf = float(page // pages_per_block)
            sel = picks[0] == jf
            for t in range(1, len(picks)):
                sel = sel | (picks[t] == jf)
            return sel

        s_own = jnp.sum(qbd * kn_ref[...], axis=1, keepdims=True) * scale
        m_tot = s_own
        for page in range(n_pages):
            m_tot = jnp.maximum(m_tot, jnp.where(page_selected(page), m_ref[page], -jnp.inf))
        e_own = jnp.exp(s_own - m_tot)
        l_tot = e_own
        o = e_own * jnp.where(diag, jnp.broadcast_to(vn_ref[...], (n_heads, width)), 0.0)
        for page in range(n_pages):
            w = jnp.where(page_selected(page), jnp.exp(m_ref[page] - m_tot), 0.0)
            l_tot = l_tot + w * l_ref[page]
            o = o + w * acc_ref[page]
        o_ref[...] = jnp.sum(o / l_tot, axis=0, keepdims=True)


def _moba_sample(p3, cache_k4, cache_v4, page_table, layer, *, q_col, k_col, v_col):
    db = p3.shape[0]
    n_pages = page_table.shape[1]
    width = cache_k4.shape[-1]
    n_heads = width // HEAD_DIM
    assert MOBA_BLOCK % PAGE_SIZE == 0 and n_pages % (MOBA_BLOCK // PAGE_SIZE) == 0
    kern = functools.partial(_moba_sample_kernel, n_pages=n_pages, n_heads=n_heads)

    def row_spec(col):
        return pl.BlockSpec((None, 1, width), lambda bi, pg, pt: (bi, 0, col))

    def page_spec():
        return pl.BlockSpec((None, None, PAGE_SIZE, width),
                            lambda bi, pg, pt: (layer, pt[bi * n_pages + pg], 0, 0))

    grid_spec = pltpu.PrefetchScalarGridSpec(
        num_scalar_prefetch=1,
        grid=(db, n_pages),
        in_specs=[row_spec(q_col), row_spec(k_col), row_spec(v_col), page_spec(), page_spec()],
        out_specs=pl.BlockSpec((None, 1, width), lambda bi, pg, pt: (bi, 0, 0)),
        scratch_shapes=[
            pltpu.VMEM((n_heads, width), F32),
            pltpu.VMEM((n_pages, n_heads, 1), F32),
            pltpu.VMEM((n_pages, n_heads, 1), F32),
            pltpu.VMEM((n_pages, n_heads, width), F32),
            pltpu.VMEM((n_pages, width), F32),
        ],
    )
    return pl.pallas_call(
        kern,
        grid_spec=grid_spec,
        out_shape=jax.ShapeDtypeStruct((db, 1, width), F32),
        compiler_params=_params("parallel", "arbitrary"),
        name="moba_sample",
    )(page_table.reshape(-1), p3, p3, p3, cache_k4, cache_v4)


def _l2norm(x):
    return x * lax.rsqrt(jnp.sum(x * x, axis=-1, keepdims=True) + 1e-6)


def _gated_rmsnorm(o, z, w):
    on = o * lax.rsqrt(jnp.mean(o * o, axis=-1, keepdims=True) + EPS) * w
    return on * _silu(z)


def _tri_inverse(a_strict):
    c = a_strict.shape[0]
    eye = (lax.broadcasted_iota(jnp.int32, (c, c), 0) == lax.broadcasted_iota(jnp.int32, (c, c), 1)).astype(F32)
    x = -a_strict
    t = eye + x
    power = 2
    while power < c:
        x = _mm3(x, x)
        t = t + _mm3(t, x)
        power *= 2
    return t


def _gdn_prompt_kernel(q_ref, k_ref, v_ref, qh_ref, kh_ref, vh_ref, z_ref, ba_ref, bat_ref,
                       cw_ref, alog_l_ref, dtb_l_ref, alog_c_ref, dtb_c_ref, norm_ref,
                       y_ref, s_out_ref, s_ref, *, n_heads, n_chunks):
    c = pl.program_id(1)
    ch = GDN_CHUNK

    @pl.when(c == 0)
    def _():
        s_ref[...] = jnp.zeros_like(s_ref)

    def conv(u_ref, halo_ref, part):
        halo = jnp.where(c == 0, 0.0, halo_ref[...])
        ext = jnp.concatenate([halo, u_ref[...]], axis=0)
        w = cw_ref[:, part * n_heads * HEAD_DIM:(part + 1) * n_heads * HEAD_DIM]
        y = None
        for i in range(CONV_W):
            shift = CONV_W - 1 - i
            tap = ext if shift == 0 else pltpu.roll(ext, shift, 0)
            term = tap[SUBLANES:, :] * w[i:i + 1, :]
            y = term if y is None else y + term
        return _silu(y)

    cq = conv(q_ref, qh_ref, 0)
    ck = conv(k_ref, kh_ref, 1)
    cv = conv(v_ref, vh_ref, 2)

    ba = ba_ref[...]
    beta_cols = _sigmoid(ba)
    g_cols = -jnp.exp(alog_l_ref[...]) * _softplus(ba + dtb_l_ref[...])
    bat = bat_ref[...]
    g_rows = -jnp.exp(alog_c_ref[...]) * _softplus(bat[n_heads:, :] + dtb_c_ref[...])

    ri = lax.broadcasted_iota(jnp.int32, (ch, ch), 0)
    ci = lax.broadcasted_iota(jnp.int32, (ch, ch), 1)
    incl = ri >= ci
    strict = ri > ci
    lower = incl.astype(BF16)
    upper = (ri <= ci).astype(BF16)
    gc_cols = _mm_exact_lhs(lower, g_cols)
    gc_rows = _mm_exact_rhs(g_rows, upper)

    for h in range(n_heads):
        sl = slice(h * HEAD_DIM, (h + 1) * HEAD_DIM)
        q = _l2norm(cq[:, sl]) * (HEAD_DIM ** -0.5)
        k = _l2norm(ck[:, sl])
        v = cv[:, sl]
        beta = beta_cols[:, h:h + 1]
        gcc = gc_cols[:, n_heads + h:n_heads + h + 1]
        gcr = gc_rows[h:h + 1, :]
        dec_incl = jnp.exp(jnp.where(incl, gcc - gcr, -jnp.inf))
        dec_strict = jnp.where(strict, dec_incl, 0.0)
        egc = jnp.exp(gcc)
        g_last = gcc[ch - 1:ch, :]
        k_dec = k * jnp.exp(g_last - gcc)
        chunk_decay = jnp.exp(g_last)

        qk_kk = _mm(jnp.concatenate([q, k], axis=0), k, NT_DIMS)
        qk = qk_kk[:ch] * dec_incl
        a_strict = beta * qk_kk[ch:] * dec_strict
        t_inv = _tri_inverse(a_strict)
        rhs = jnp.concatenate([beta * v, (beta * egc) * k], axis=1)
        sol = _mm3(t_inv, rhs)
        u_v = sol[:, :HEAD_DIM]
        w = sol[:, HEAD_DIM:]

        s = s_ref[h]
        wq_s = _mm(jnp.concatenate([w, q * egc], axis=0), s)
        u = u_v - wq_s[:ch]
        o = wq_s[ch:] + _mm(qk, u)
        s_new = chunk_decay * s + _mm(k_dec, u, TN_DIMS)
        s_ref[h] = s_new
        y_ref[:, sl] = _gated_rmsnorm(o, z_ref[:, sl], norm_ref[...])

    @pl.when(c == n_chunks - 1)
    def _():
        s_out_ref[...] = s_ref[...]


def _gdn_prompt(p3, ba3, bat3, conv_w, alog_l, dtb_l, alog_c, dtb_c, norm, *, qkv_col, z_col):
    b, t, _ = p3.shape
    n_heads = 8
    width = n_heads * HEAD_DIM
    ch = GDN_CHUNK
    n_chunks = t // ch
    halo_per_chunk = ch // SUBLANES
    kern = functools.partial(_gdn_prompt_kernel, n_heads=n_heads, n_chunks=n_chunks)

    def tile(col):
        return pl.BlockSpec((None, ch, width), lambda bi, c: (bi, c, col))

    def halo(col):
        return pl.BlockSpec((None, SUBLANES, width),
                            lambda bi, c: (bi, jnp.maximum(c * halo_per_chunk - 1, 0), col))

    def const(shape):
        return pl.BlockSpec(shape, lambda bi, c: tuple(0 for _ in shape))

    return pl.pallas_call(
        kern,
        grid=(b, n_chunks),
        in_specs=[
            tile(qkv_col), tile(qkv_col + 1), tile(qkv_col + 2),
            halo(qkv_col), halo(qkv_col + 1), halo(qkv_col + 2),
            tile(z_col),
            pl.BlockSpec((None, ch, HEAD_DIM), lambda bi, c: (bi, c, 0)),
            pl.BlockSpec((None, 2 * SUBLANES, ch), lambda bi, c: (bi * n_chunks + c, 0, 0)),
            const((CONV_W, 3 * width)),
            const((1, HEAD_DIM)), const((1, HEAD_DIM)),
            const((n_heads, 1)), const((n_heads, 1)),
            const((1, HEAD_DIM)),
        ],
        out_specs=[
            pl.BlockSpec((None, ch, width), lambda bi, c: (bi, c, 0)),
            pl.BlockSpec((None, n_heads, HEAD_DIM, HEAD_DIM), lambda bi, c: (bi, 0, 0, 0)),
        ],
        out_shape=[
            jax.ShapeDtypeStruct((b, t, width), F32),
            jax.ShapeDtypeStruct((b, n_heads, HEAD_DIM, HEAD_DIM), F32),
        ],
        scratch_shapes=[pltpu.VMEM((n_heads, HEAD_DIM, HEAD_DIM), F32)],
        compiler_params=_params("parallel", "arbitrary"),
        name="gdn_prompt",
    )(p3, p3, p3, p3, p3, p3, p3, ba3, bat3, conv_w, alog_l, dtb_l, alog_c, dtb_c, norm)


def _gdn_sample_kernel(qkv_ref, conv_ref, z_ref, ba_ref, cw_ref, alog_l_ref, dtb_l_ref, norm_ref, s_ref,
                       y_ref, s_out_ref, *, n_heads):
    width3 = qkv_ref.shape[1]
    width = width3 // 3
    rows = qkv_ref.shape[0]
    w = cw_ref[...]
    y = conv_ref[:, 0:width3] * w[0:1, :]
    for i in range(1, CONV_W - 1):
        y = y + conv_ref[:, i * width3:(i + 1) * width3] * w[i:i + 1, :]
    y = y + qkv_ref[...] * w[CONV_W - 1:CONV_W, :]
    c = _silu(y)

    ba = ba_ref[...]
    beta_cols = _sigmoid(ba)
    g_cols = -jnp.exp(alog_l_ref[...]) * _softplus(ba + dtb_l_ref[...])
    decay_cols = jnp.exp(g_cols)
    eye = (lax.broadcasted_iota(jnp.int32, (HEAD_DIM, HEAD_DIM), 0)
           == lax.broadcasted_iota(jnp.int32, (HEAD_DIM, HEAD_DIM), 1)).astype(BF16)

    for h in range(n_heads):
        q = _l2norm(c[:, h * HEAD_DIM:(h + 1) * HEAD_DIM]) * (HEAD_DIM ** -0.5)
        k = _l2norm(c[:, width + h * HEAD_DIM:width + (h + 1) * HEAD_DIM])
        v = c[:, 2 * width + h * HEAD_DIM:2 * width + (h + 1) * HEAD_DIM]
        q_t = _mm_exact_lhs(eye, q, NT_DIMS)
        k_t = _mm_exact_lhs(eye, k, NT_DIMS)
        for r in range(rows):
            s = decay_cols[r:r + 1, n_heads + h:n_heads + h + 1] * s_ref[r, h]
            k_col = k_t[:, r:r + 1]
            u = beta_cols[r:r + 1, h:h + 1] * (v[r:r + 1, :] - jnp.sum(k_col * s, axis=0, keepdims=True))
            s = s + k_col * u
            s_out_ref[r, h] = s
            o = jnp.sum(q_t[:, r:r + 1] * s, axis=0, keepdims=True)
            zr = z_ref[r:r + 1, h * HEAD_DIM:(h + 1) * HEAD_DIM]
            y_ref[r:r + 1, h * HEAD_DIM:(h + 1) * HEAD_DIM] = _gated_rmsnorm(o, zr, norm_ref[...])


def _gdn_sample(p2, conv_prev2, ba2, state, conv_w, alog_l, dtb_l, norm, *, qkv_col, z_col):
    db = p2.shape[0]
    n_heads = state.shape[1]
    width = n_heads * HEAD_DIM
    rows = SUBLANES
    kern = functools.partial(_gdn_sample_kernel, n_heads=n_heads)

    def const(shape):
        return pl.BlockSpec(shape, lambda i: tuple(0 for _ in shape))

    return pl.pallas_call(
        kern,
        grid=(db // rows,),
        in_specs=[
            pl.BlockSpec((rows, 3 * width), lambda i: (i, qkv_col // 3)),
            pl.BlockSpec((rows, (CONV_W - 1) * 3 * width), lambda i: (i, 0)),
            pl.BlockSpec((rows, width), lambda i: (i, z_col)),
            pl.BlockSpec((rows, HEAD_DIM), lambda i: (i, 0)),
            const((CONV_W, 3 * width)),
            const((1, HEAD_DIM)), const((1, HEAD_DIM)), const((1, HEAD_DIM)),
            pl.BlockSpec((rows, n_heads, HEAD_DIM, HEAD_DIM), lambda i: (i, 0, 0, 0)),
        ],
        out_specs=[
            pl.BlockSpec((rows, width), lambda i: (i, 0)),
            pl.BlockSpec((rows, n_heads, HEAD_DIM, HEAD_DIM), lambda i: (i, 0, 0, 0)),
        ],
        out_shape=[
            jax.ShapeDtypeStruct((db, width), F32),
            jax.ShapeDtypeStruct(state.shape, F32),
        ],
        compiler_params=_params("parallel"),
        name="gdn_sample",
    )(p2, conv_prev2, p2, ba2, conv_w, alog_l, dtb_l, norm, state)


def _merge_kernel(x_ref, yg_ref, ym_ref, sg_ref, sm_ref, wpg_ref, wpm_ref, wo_ref, o_ref):
    yg = jnp.dot(yg_ref[...].astype(BF16), wpg_ref[...], preferred_element_type=F32)
    ym = jnp.dot(ym_ref[...].astype(BF16), wpm_ref[...], preferred_element_type=F32)
    mixed = sg_ref[...] * yg + sm_ref[...] * ym
    o_ref[...] = x_ref[...] + jnp.dot(mixed.astype(BF16), wo_ref[...], preferred_element_type=F32)


def _merge(x, yg, ym, p, w_pg, w_pm, w_o, *, tm, sg_col, sm_col):
    m, d = x.shape

    def rows(col):
        return pl.BlockSpec((tm, d), lambda i: (i, col))

    def weight():
        return pl.BlockSpec((d, d), lambda i: (0, 0))

    return pl.pallas_call(
        _merge_kernel,
        grid=(m // tm,),
        in_specs=[rows(0), rows(0), rows(0), rows(sg_col), rows(sm_col), weight(), weight(), weight()],
        out_specs=rows(0),
        out_shape=jax.ShapeDtypeStruct((m, d), F32),
        compiler_params=_params("parallel"),
        name="merge",
    )(x, yg, ym, p, p, w_pg, w_pm, w_o)


def _ffn_kernel(x_ref, ln_ref, wg_ref, wu_ref, wd_ref, lnf_ref, o_ref, xn_ref, acc_ref, *, final):
    j = pl.program_id(1)

    @pl.when(j == 0)
    def _():
        x = x_ref[...]
        y = x * lax.rsqrt(jnp.mean(x * x, axis=-1, keepdims=True) + EPS)
        xn_ref[...] = (y * ln_ref[...]).astype(BF16)
        acc_ref[...] = jnp.zeros_like(acc_ref)

    xn = xn_ref[...]
    g = jnp.dot(xn, wg_ref[...], preferred_element_type=F32)
    u = jnp.dot(xn, wu_ref[...], preferred_element_type=F32)
    acc_ref[...] += jnp.dot((_silu(g) * u).astype(BF16), wd_ref[...], preferred_element_type=F32)

    @pl.when(j == pl.num_programs(1) - 1)
    def _():
        out = x_ref[...] + acc_ref[...]
        if final:
            out = out * lax.rsqrt(jnp.mean(out * out, axis=-1, keepdims=True) + EPS) * lnf_ref[...]
        o_ref[...] = out


def _ffn(x, ln, w_gu, w_dn, ln_final, *, tm, tf, final):
    m, d = x.shape
    d_ff = w_dn.shape[0]
    n_f = d_ff // tf
    return pl.pallas_call(
        functools.partial(_ffn_kernel, final=final),
        grid=(m // tm, n_f),
        in_specs=[
            pl.BlockSpec((tm, d), lambda i, j: (i, 0)),
            pl.BlockSpec((1, d), lambda i, j: (0, 0)),
            pl.BlockSpec((d, tf), lambda i, j: (0, j)),
            pl.BlockSpec((d, tf), lambda i, j: (0, n_f + j)),
            pl.BlockSpec((tf, d), lambda i, j: (j, 0)),
            pl.BlockSpec((1, d), lambda i, j: (0, 0)),
        ],
        out_specs=pl.BlockSpec((tm, d), lambda i, j: (i, 0)),
        out_shape=jax.ShapeDtypeStruct((m, d), F32),
        scratch_shapes=[pltpu.VMEM((tm, d), BF16), pltpu.VMEM((tm, d), F32)],
        compiler_params=_params("parallel", "arbitrary"),
        name="ffn",
    )(x, ln, w_gu, w_gu, w_dn, ln_final)


def _rope_tables(pos):
    half = HEAD_DIM // 2
    inv = jnp.exp(-math.log(ROPE_THETA) * jnp.arange(half, dtype=F32) / half)
    ang = pos.astype(F32)[:, None] * inv[None, :]
    cos = jnp.cos(ang)
    sin = jnp.sin(ang)
    return jnp.concatenate([cos, cos], axis=-1), jnp.concatenate([-sin, sin], axis=-1)


def _lane_row(vec, offset):
    return jnp.zeros((1, HEAD_DIM), F32).at[0, offset:offset + vec.shape[0]].set(vec.astype(F32))


def kernel(x_prompt, x_sample, cache_k, cache_v, state_ssm, state_conv, page_table, ln_mix, ln_ffn, w_in,
           conv_w, a_log, dt_bias, gdn_norm, w_branch_gdn, w_branch_moba, w_out, w_gate_up, w_down, ln_final):
    b, t, d = x_prompt.shape
    db, ts, _ = x_sample.shape
    depth = w_in.shape[0]
    n_heads = d // HEAD_DIM
    width = n_heads * HEAD_DIM
    n_pages = page_table.shape[1]
    n_pool = cache_k.shape[1]
    past = n_pages * PAGE_SIZE
    d_ff = w_down.shape[1]
    assert ts == 1 and n_heads == 8 and width == d
    assert t % MOBA_BLOCK == 0 and t % 1024 == 0 and db % SUBLANES == 0

    c_ba = 7 * width
    c_g = c_ba + 2 * n_heads
    col_q, col_k, col_v, col_qkv, col_z, col_sg, col_sm = 0, 1, 2, 3, 6, 7, 8

    cos_p, sin_p = _rope_tables(jnp.arange(t, dtype=jnp.int32))
    cos_s, sin_s = _rope_tables(jnp.full((db,), past, dtype=jnp.int32))

    cache_k4 = cache_k.reshape(depth, n_pool, PAGE_SIZE, width)
    cache_v4 = cache_v.reshape(depth, n_pool, PAGE_SIZE, width)

    tm_p = 1024
    tf = d_ff // 2 if (d_ff // 2) % HEAD_DIM == 0 else d_ff
    xp = x_prompt.reshape(b * t, d)
    xs = x_sample.reshape(db * ts, d)
    outs = {name: [] for name in ("kp", "vp", "sp", "cp", "ks", "vs", "ss", "cs")}
    for l in range(depth):
        last = l == depth - 1
        w_main = jnp.concatenate([w_in[l, :, :c_ba], w_in[l, :, c_g:]], axis=1).astype(BF16)
        w_ba = jnp.pad(w_in[l, :, c_ba:c_g], ((0, 0), (0, HEAD_DIM - 2 * n_heads))).astype(BF16)
        w_bat = w_in[l, :, c_ba:c_g].T.astype(BF16)
        ln_m = ln_mix[l].reshape(1, d)
        ln_f = ln_ffn[l].reshape(1, d)
        alog_l = _lane_row(a_log[l], n_heads)
        dtb_l = _lane_row(dt_bias[l], n_heads)
        alog_c = a_log[l].reshape(n_heads, 1).astype(F32)
        dtb_c = dt_bias[l].reshape(n_heads, 1).astype(F32)
        norm = gdn_norm[l].reshape(1, HEAD_DIM)
        w_pg = w_branch_gdn[l].astype(BF16)
        w_pm = w_branch_moba[l].astype(BF16)
        w_o = w_out[l].astype(BF16)
        w_gu = w_gate_up[l].astype(BF16)
        w_dn = w_down[l].astype(BF16)
        proj = functools.partial(_in_proj, tn=width, n_rope_tiles=2, first_sigmoid_tile=col_sg)

        p, ba, bat = proj(xp, ln_m, w_main, w_ba, w_bat, cos_p, sin_p, tm=tm_p)
        p3 = p.reshape(b, t, -1)
        ym = _moba_prompt(p3, q_col=col_q * n_heads, k_col=col_k * n_heads, v_col=col_v * n_heads)
        bat3 = bat.reshape(2 * n_heads, b * t // GDN_CHUNK, GDN_CHUNK).transpose(1, 0, 2)
        yg, s_fin = _gdn_prompt(p3, ba.reshape(b, t, HEAD_DIM), bat3, conv_w[l], alog_l, dtb_l, alog_c, dtb_c,
                                norm, qkv_col=col_qkv, z_col=col_z)
        x1 = _merge(xp, yg.reshape(b * t, width), ym.reshape(b * t, width), p, w_pg, w_pm, w_o,
                    tm=tm_p // 2, sg_col=col_sg, sm_col=col_sm)
        xp = _ffn(x1, ln_f, w_gu, w_dn, ln_final.reshape(1, d), tm=tm_p, tf=tf, final=last)
        outs["kp"].append(p3[:, :, col_k * width:(col_k + 1) * width])
        outs["vp"].append(p3[:, :, col_v * width:(col_v + 1) * width])
        outs["sp"].append(s_fin)
        outs["cp"].append(p3[:, t - (CONV_W - 1):, col_qkv * width:(col_qkv + 3) * width])

        p, ba, bat = proj(xs, ln_m, w_main, w_ba, w_bat, cos_s, sin_s, tm=db)
        ps3 = p.reshape(db, 1, -1)
        ym = _moba_sample(ps3, cache_k4, cache_v4, page_table, l, q_col=col_q, k_col=col_k, v_col=col_v)
        yg, s_new = _gdn_sample(p, state_conv[l].reshape(db, -1), ba, state_ssm[l], conv_w[l], alog_l, dtb_l,
                                norm, qkv_col=col_qkv, z_col=col_z)
        x1 = _merge(xs, yg, ym.reshape(db, width), p, w_pg, w_pm, w_o, tm=db, sg_col=col_sg, sm_col=col_sm)
        xs = _ffn(x1, ln_f, w_gu, w_dn, ln_final.reshape(1, d), tm=db, tf=tf, final=last)
        qkv_new = p[:, col_qkv * width:(col_qkv + 3) * width]
        outs["ks"].append(p[:, col_k * width:(col_k + 1) * width])
        outs["vs"].append(p[:, col_v * width:(col_v + 1) * width])
        outs["ss"].append(s_new)
        outs["cs"].append(jnp.concatenate([state_conv[l][:, 1:], qkv_new[:, None, :]], axis=1))

    n_tpages = t // PAGE_SIZE
    return (
        xp.reshape(b, t, d),
        xs.reshape(db, ts, d),
        jnp.stack(outs["kp"]).reshape(depth, b, n_tpages, PAGE_SIZE, n_heads, HEAD_DIM),
        jnp.stack(outs["vp"]).reshape(depth, b, n_tpages, PAGE_SIZE, n_heads, HEAD_DIM),
        jnp.stack(outs["sp"]),
        jnp.stack(outs["cp"]),
        jnp.stack(outs["ks"]).reshape(depth, db, ts, n_heads, HEAD_DIM),
        jnp.stack(outs["vs"]).reshape(depth, db, ts, n_heads, HEAD_DIM),
        jnp.stack(outs["ss"]),
        jnp.stack(outs["cs"]),
    )
```

```python
import functools
import math

import jax
import jax.numpy as jnp
import numpy as np
from jax import lax
from jax.experimental import pallas as pl
from jax.experimental.pallas import tpu as pltpu

F32 = jnp.float32
BF16 = jnp.bfloat16

HEAD_DIM = 128
MOBA_BLOCK = 256
MOBA_TOPK = 3
MOBA_PAIR_UNROLL = 4
GDN_CHUNK = 64
GDN_TILE = 2 * GDN_CHUNK
CONV_W = 4
PAGE_SIZE = 128
ROPE_THETA = 10000.0
EPS = 1e-6
SUBLANES = 8
VMEM_LIMIT_BYTES = 50 * 1024 * 1024
LOG2_E = 1.4426950408889634

NT_DIMS = (((1,), (1,)), ((), ()))
TN_DIMS = (((0,), (0,)), ((), ()))


def _params(*semantics):
    return pltpu.CompilerParams(dimension_semantics=semantics, vmem_limit_bytes=VMEM_LIMIT_BYTES)


def _mm(a, b, dims=None):
    a = a.astype(BF16)
    b = b.astype(BF16)
    if dims is None:
        return jnp.dot(a, b, preferred_element_type=F32)
    return lax.dot_general(a, b, dims, preferred_element_type=F32)


def _split3(x):
    hi = x.astype(BF16)
    r1 = x - hi.astype(F32)
    mid = r1.astype(BF16)
    lo = (r1 - mid.astype(F32)).astype(BF16)
    return hi, mid, lo


def _mm_exact_rhs(a, b_exact, dims=None):
    out = None
    for part in _split3(a):
        t = _mm(part, b_exact, dims)
        out = t if out is None else out + t
    return out


def _mm_exact_lhs(a_exact, b, dims=None):
    out = None
    for part in _split3(b):
        t = _mm(a_exact, part, dims)
        out = t if out is None else out + t
    return out


def _sigmoid(x):
    return 1.0 / (1.0 + jnp.exp(-x))


def _silu(x):
    return x * _sigmoid(x)


def _softplus(x):
    return jnp.maximum(x, 0.0) + jnp.log1p(jnp.exp(-jnp.abs(x)))


def _in_proj_kernel(x_ref, ln_ref, w_ref, wba_ref, wbat_ref, cos_ref, sin_ref,
                    p_ref, ba_ref, bat_ref, xn_ref, *, n_rope_tiles, first_sigmoid_tile):
    j = pl.program_id(1)

    @pl.when(j == 0)
    def _():
        x = x_ref[...]
        y = x * lax.rsqrt(jnp.mean(x * x, axis=-1, keepdims=True) + EPS)
        xn = (y * ln_ref[...]).astype(BF16)
        xn_ref[...] = xn
        ba_ref[...] = jnp.dot(xn, wba_ref[...], preferred_element_type=F32)
        bat_ref[...] = lax.dot_general(wbat_ref[...], xn, NT_DIMS, preferred_element_type=F32)

    acc = jnp.dot(xn_ref[...], w_ref[...], preferred_element_type=F32)

    @pl.when(j < n_rope_tiles)
    def _():
        cos = cos_ref[...]
        sin = sin_ref[...]
        for h in range(acc.shape[1] // HEAD_DIM):
            sl = slice(h * HEAD_DIM, (h + 1) * HEAD_DIM)
            xh = acc[:, sl]
            p_ref[:, sl] = xh * cos + pltpu.roll(xh, HEAD_DIM // 2, 1) * sin

    @pl.when((j >= n_rope_tiles) & (j < first_sigmoid_tile))
    def _():
        p_ref[...] = acc

    @pl.when(j >= first_sigmoid_tile)
    def _():
        p_ref[...] = _sigmoid(acc)


def _in_proj(x, ln, w_main, w_ba, w_bat, cos, sin, *, tm, tn, n_rope_tiles, first_sigmoid_tile):
    m, d = x.shape
    n = w_main.shape[1]
    rope_blocks = cos.shape[0] // tm
    kern = functools.partial(_in_proj_kernel, n_rope_tiles=n_rope_tiles,
                             first_sigmoid_tile=first_sigmoid_tile)
    return pl.pallas_call(
        kern,
        grid=(m // tm, n // tn),
        in_specs=[
            pl.BlockSpec((tm, d), lambda i, j: (i, 0)),
            pl.BlockSpec((1, d), lambda i, j: (0, 0)),
            pl.BlockSpec((d, tn), lambda i, j: (0, j)),
            pl.BlockSpec((d, HEAD_DIM), lambda i, j: (0, 0)),
            pl.BlockSpec((2 * SUBLANES, d), lambda i, j: (0, 0)),
            pl.BlockSpec((tm, HEAD_DIM), lambda i, j: (i % rope_blocks, 0)),
            pl.BlockSpec((tm, HEAD_DIM), lambda i, j: (i % rope_blocks, 0)),
        ],
        out_specs=[
            pl.BlockSpec((tm, tn), lambda i, j: (i, j)),
            pl.BlockSpec((tm, HEAD_DIM), lambda i, j: (i, 0)),
            pl.BlockSpec((2 * SUBLANES, tm), lambda i, j: (0, i)),
        ],
        out_shape=[
            jax.ShapeDtypeStruct((m, n), F32),
            jax.ShapeDtypeStruct((m, HEAD_DIM), F32),
            jax.ShapeDtypeStruct((2 * SUBLANES, m), F32),
        ],
        scratch_shapes=[pltpu.VMEM((tm, d), BF16)],
        compiler_params=_params("parallel", "arbitrary"),
        name="in_proj",
    )(x, ln, w_main, w_ba, w_bat, cos, sin)


def _top_blocks_cols(gate, n_valid, n_top):
    n, cols = gate.shape
    blk = lax.broadcasted_iota(jnp.int32, (n, cols), 0).astype(F32)
    g = jnp.where(blk < n_valid, gate, -jnp.inf)
    picks = []
    for _ in range(n_top):
        m = jnp.max(g, axis=0, keepdims=True)
        cand = jnp.where((g == m) & (m > -jnp.inf), blk, float(n))
        idx = jnp.min(cand, axis=0, keepdims=True)
        picks.append(idx)
        g = jnp.where(blk == idx, -jnp.inf, g)
    return picks


def _moba_prompt_kernel(pj_ref, pq_ref, q_ref, k_ref, v_ref, o_ref,
                        kb_ref, vt_ref, qt_ref, kmean_ref, picks_ref, m_ref, l_ref, acc_ref,
                        *, n_blocks, n_groups, unroll):
    blk = MOBA_BLOCK
    n_top = min(MOBA_TOPK, n_blocks)
    score_scale = (HEAD_DIM ** -0.5) * LOG2_E

    def block_rows(j):
        return pl.ds(pl.multiple_of(j * blk, blk), blk)

    def prepare_block(j, carry):
        kj = k_ref[block_rows(j), :]
        kb_ref[block_rows(j), :] = kj.astype(BF16)
        vt_ref[j] = v_ref[block_rows(j), :].T.astype(BF16)
        kmean_ref[pl.ds(j, 1), :] = jnp.mean(kj, axis=0, keepdims=True)
        return carry

    lax.fori_loop(0, n_blocks, prepare_block, 0)

    key = lax.broadcasted_iota(jnp.int32, (blk, blk), 0)
    qry = lax.broadcasted_iota(jnp.int32, (blk, blk), 1)
    no_pick = jnp.full((SUBLANES - n_top, blk), -1.0, F32)

    def init_tile(qb):
        q_t = q_ref[block_rows(qb), :].T
        gate = jnp.dot(kmean_ref[...], q_t, preferred_element_type=F32, precision=lax.Precision.HIGHEST)
        picks = _top_blocks_cols(gate, qb.astype(F32), n_top)
        picks_ref[qb] = jnp.concatenate(picks + [no_pick], axis=0)
        q16 = q_t.astype(BF16)
        qt_ref[qb] = q16
        s = jnp.dot(kb_ref[block_rows(qb), :], q16, preferred_element_type=F32) * score_scale
        s = jnp.where(key <= qry, s, -jnp.inf)
        m = jnp.max(s, axis=0, keepdims=True)
        p = jnp.exp2(s - m)
        m_ref[qb] = m
        l_ref[qb] = jnp.sum(p, axis=0, keepdims=True)
        acc_ref[qb] = jnp.dot(vt_ref[qb], p.astype(BF16), preferred_element_type=F32)

    def init_tiles(i, carry):
        for u in range(unroll):
            init_tile(i * unroll + u)
        return carry

    lax.fori_loop(0, n_blocks // unroll, init_tiles, 0)

    picks_ref[n_blocks] = jnp.full((SUBLANES, blk), -1.0, F32)
    qt_ref[n_blocks] = jnp.zeros((HEAD_DIM, blk), BF16)
    m_ref[n_blocks] = jnp.zeros((1, blk), F32)
    l_ref[n_blocks] = jnp.zeros((1, blk), F32)
    acc_ref[n_blocks] = jnp.zeros((HEAD_DIM, blk), F32)

    def pair_group(g, carry):
        lanes = range(unroll)
        js = [pj_ref[g * unroll + u] for u in lanes]
        qs = [pq_ref[g * unroll + u] for u in lanes]
        m_old = [m_ref[qs[u]] for u in lanes]
        l_old = [l_ref[qs[u]] for u in lanes]
        acc_old = [acc_ref[qs[u]] for u in lanes]
        picks = [picks_ref[qs[u]] for u in lanes]
        s = [jnp.dot(kb_ref[block_rows(js[u]), :], qt_ref[qs[u]], preferred_element_type=F32) * score_scale
             for u in lanes]
        sel = []
        for u in lanes:
            jf = js[u].astype(F32)
            hit = picks[u][0:1, :] == jf
            for t in range(1, n_top):
                hit = hit | (picks[u][t:t + 1, :] == jf)
            sel.append(hit)
        m_blk = [jnp.where(sel[u], jnp.max(s[u], axis=0, keepdims=True), -jnp.inf) for u in lanes]
        m_new = [jnp.maximum(m_old[u], m_blk[u]) for u in lanes]
        alpha = [jnp.exp2(m_old[u] - m_new[u]) for u in lanes]
        shift = [jnp.where(sel[u], m_new[u], jnp.inf) for u in lanes]
        p = [jnp.exp2(s[u] - shift[u]) for u in lanes]
        l_new = [alpha[u] * l_old[u] + jnp.sum(p[u], axis=0, keepdims=True) for u in lanes]
        acc_new = [alpha[u] * acc_old[u]
                   + jnp.dot(vt_ref[js[u]], p[u].astype(BF16), preferred_element_type=F32) for u in lanes]
        for u in lanes:
            m_ref[qs[u]] = m_new[u]
            l_ref[qs[u]] = l_new[u]
            acc_ref[qs[u]] = acc_new[u]
        return carry

    lax.fori_loop(0, n_groups, pair_group, 0)

    def finish_tile(qb, carry):
        o_ref[block_rows(qb), :] = (acc_ref[qb] / l_ref[qb]).T
        return carry

    lax.fori_loop(0, n_blocks, finish_tile, 0)


def _moba_pair_table(n_blocks, unroll):
    pj, pq = [], []
    for j in range(n_blocks - 1):
        tiles = list(range(j + 1, n_blocks))
        tiles += [n_blocks] * (-len(tiles) % unroll)
        pj += [j if q < n_blocks else 0 for q in tiles]
        pq += tiles
    if not pj:
        pj, pq = [0] * unroll, [n_blocks] * unroll
    return np.asarray(pj, np.int32), np.asarray(pq, np.int32)


def _moba_prompt(p3, *, q_col, k_col, v_col):
    b, t, _ = p3.shape
    n_heads = 8
    nb = t // MOBA_BLOCK
    unroll = MOBA_PAIR_UNROLL if nb % MOBA_PAIR_UNROLL == 0 else 1
    pair_j, pair_q = _moba_pair_table(nb, unroll)
    kern = functools.partial(_moba_prompt_kernel, n_blocks=nb, n_groups=len(pair_j) // unroll, unroll=unroll)

    def seq(col):
        return pl.BlockSpec((None, t, HEAD_DIM), lambda bi, h, pj, pq: (bi, 0, col + h))

    grid_spec = pltpu.PrefetchScalarGridSpec(
        num_scalar_prefetch=2,
        grid=(b, n_heads),
        in_specs=[seq(q_col), seq(k_col), seq(v_col)],
        out_specs=seq(0),
        scratch_shapes=[
            pltpu.VMEM((t, HEAD_DIM), BF16),
            pltpu.VMEM((nb, HEAD_DIM, MOBA_BLOCK), BF16),
            pltpu.VMEM((nb + 1, HEAD_DIM, MOBA_BLOCK), BF16),
            pltpu.VMEM((nb, HEAD_DIM), F32),
            pltpu.VMEM((nb + 1, SUBLANES, MOBA_BLOCK), F32),
            pltpu.VMEM((nb + 1, 1, MOBA_BLOCK), F32),
            pltpu.VMEM((nb + 1, 1, MOBA_BLOCK), F32),
            pltpu.VMEM((nb + 1, HEAD_DIM, MOBA_BLOCK), F32),
        ],
    )
    return pl.pallas_call(
        kern,
        grid_spec=grid_spec,
        out_shape=jax.ShapeDtypeStruct((b, t, n_heads * HEAD_DIM), F32),
        compiler_params=_params("parallel", "parallel"),
        name="moba_prompt",
    )(jnp.asarray(pair_j), jnp.asarray(pair_q), p3, p3, p3)


def _moba_sample_kernel(pt_ref, q_ref, kn_ref, vn_ref, *refs, n_pages):
    del pt_ref
    k_refs = refs[:n_pages]
    v_refs = refs[n_pages:2 * n_pages]
    o_ref = refs[2 * n_pages]
    scale = HEAD_DIM ** -0.5
    pages_per_block = MOBA_BLOCK // PAGE_SIZE
    n_past = n_pages // pages_per_block
    q = q_ref[...]

    gates, ms, ls, accs = [], [], [], []
    for j in range(n_past):
        pages = range(j * pages_per_block, (j + 1) * pages_per_block)
        ks = [k_refs[pg][...] for pg in pages]
        ss = [jnp.sum(k * q[None], axis=-1, keepdims=True) * scale for k in ks]
        ksum = ks[0].sum(axis=0)
        m = jnp.max(ss[0], axis=0)
        for k, s in zip(ks[1:], ss[1:]):
            ksum = ksum + k.sum(axis=0)
            m = jnp.maximum(m, jnp.max(s, axis=0))
        l = None
        acc = None
        for pg, s in zip(pages, ss):
            p = jnp.exp(s - m[None])
            pl_sum = p.sum(axis=0)
            pv = (p * v_refs[pg][...]).sum(axis=0)
            l = pl_sum if l is None else l + pl_sum
            acc = pv if acc is None else acc + pv
        gates.append(jnp.sum(q * (ksum * (1.0 / MOBA_BLOCK)), axis=-1, keepdims=True))
        ms.append(m)
        ls.append(l)
        accs.append(acc)

    n_top = min(MOBA_TOPK, n_past + 1)
    remaining = list(gates)
    selected = [jnp.zeros_like(gates[0], dtype=jnp.bool_) for _ in range(n_past)]
    for _ in range(n_top):
        best = remaining[0]
        for g in remaining[1:]:
            best = jnp.maximum(best, g)
        idx = jnp.full_like(best, float(n_past))
        for j in reversed(range(n_past)):
            idx = jnp.where(remaining[j] == best, float(j), idx)
        idx = jnp.where(best > -jnp.inf, idx, float(n_past))
        for j in range(n_past):
            hit = idx == float(j)
            selected[j] = selected[j] | hit
            remaining[j] = jnp.where(hit, -jnp.inf, remaining[j])

    s_own = jnp.sum(q * kn_ref[...], axis=-1, keepdims=True) * scale
    m_tot = s_own
    for j in range(n_past):
        m_tot = jnp.maximum(m_tot, jnp.where(selected[j], ms[j], -jnp.inf))
    e_own = jnp.exp(s_own - m_tot)
    l_tot = e_own
    o = e_own * vn_ref[...]
    for j in range(n_past):
        w = jnp.where(selected[j], jnp.exp(ms[j] - m_tot), 0.0)
        l_tot = l_tot + w * ls[j]
        o = o + w * accs[j]
    o_ref[...] = o / l_tot


def _moba_sample(ph, cache_k, cache_v, page_table, layer, *, q_row, k_row, v_row):
    db = ph.shape[0]
    n_pages = page_table.shape[1]
    n_heads = cache_k.shape[3]
    assert MOBA_BLOCK % PAGE_SIZE == 0 and n_pages % (MOBA_BLOCK // PAGE_SIZE) == 0
    assert n_heads == SUBLANES
    kern = functools.partial(_moba_sample_kernel, n_pages=n_pages)

    def row_spec(row):
        return pl.BlockSpec((None, n_heads, HEAD_DIM), lambda bi, pt: (bi, row, 0))

    def page_spec(page):
        return pl.BlockSpec((None, None, PAGE_SIZE, n_heads, HEAD_DIM),
                            lambda bi, pt: (layer, pt[bi * n_pages + page], 0, 0, 0))

    grid_spec = pltpu.PrefetchScalarGridSpec(
        num_scalar_prefetch=1,
        grid=(db,),
        in_specs=([row_spec(q_row), row_spec(k_row), row_spec(v_row)]
                  + [page_spec(pg) for pg in range(n_pages)] * 2),
        out_specs=pl.BlockSpec((None, n_heads, HEAD_DIM), lambda bi, pt: (bi, 0, 0)),
    )
    return pl.pallas_call(
        kern,
        grid_spec=grid_spec,
        out_shape=jax.ShapeDtypeStruct((db, n_heads, HEAD_DIM), F32),
        compiler_params=_params("parallel"),
        name="moba_sample",
    )(page_table.reshape(-1), ph, ph, ph, *([cache_k] * n_pages), *([cache_v] * n_pages))


def _l2norm(x):
    return x * lax.rsqrt(jnp.sum(x * x, axis=-1, keepdims=True) + 1e-6)


def _gated_rmsnorm(o, z, w):
    on = o * lax.rsqrt(jnp.mean(o * o, axis=-1, keepdims=True) + EPS) * w
    return on * _silu(z)


def _gdn_prompt_kernel(q_ref, k_ref, v_ref, qh_ref, kh_ref, vh_ref, z_ref, ba_ref, bat_ref,
                       cw_ref, alog_l_ref, dtb_l_ref, alog_c_ref, dtb_c_ref, norm_ref,
                       y_ref, s_out_ref, s_ref, *, n_heads, n_tiles):
    c = pl.program_id(1)
    ch = GDN_CHUNK
    tile = GDN_TILE
    n_sub = tile // ch
    heads = range(n_heads)

    @pl.when(c == 0)
    def _():
        s_ref[...] = jnp.zeros_like(s_ref)

    def conv(u_ref, halo_ref, part):
        halo = jnp.where(c == 0, 0.0, halo_ref[...])
        ext = jnp.concatenate([halo, u_ref[...]], axis=0)
        w = cw_ref[:, part * n_heads * HEAD_DIM:(part + 1) * n_heads * HEAD_DIM]
        y = None
        for i in range(CONV_W):
            shift = CONV_W - 1 - i
            tap = ext if shift == 0 else pltpu.roll(ext, shift, 0)
            term = tap[SUBLANES:, :] * w[i:i + 1, :]
            y = term if y is None else y + term
        return _silu(y)

    cq = conv(q_ref, qh_ref, 0)
    ck = conv(k_ref, kh_ref, 1)
    cv = conv(v_ref, vh_ref, 2)

    ba = ba_ref[...]
    beta_cols = _sigmoid(ba)
    g_cols = -jnp.exp(alog_l_ref[...]) * _softplus(ba + dtb_l_ref[...])
    g_rows = -jnp.exp(alog_c_ref[...]) * _softplus(bat_ref[n_heads:, :] + dtb_c_ref[...])

    ri = lax.broadcasted_iota(jnp.int32, (tile, tile), 0)
    ci = lax.broadcasted_iota(jnp.int32, (tile, tile), 1)
    same_chunk = (ri // ch) == (ci // ch)
    incl = same_chunk & (ri >= ci)
    strict = same_chunk & (ri > ci)
    eye = (ri == ci).astype(F32)
    gc_cols = _mm_exact_lhs(incl.astype(BF16), g_cols)
    gc_rows = _mm_exact_rhs(g_rows, (same_chunk & (ri <= ci)).astype(BF16))
    row_chunk = lax.broadcasted_iota(jnp.int32, (tile, 1), 0) // ch

    sls = [slice(h * HEAD_DIM, (h + 1) * HEAD_DIM) for h in heads]
    q = [_l2norm(cq[:, sl]) * (HEAD_DIM ** -0.5) for sl in sls]
    k = [_l2norm(ck[:, sl]) for sl in sls]
    v = [cv[:, sl] for sl in sls]
    beta = [beta_cols[:, h:h + 1] for h in heads]
    gcc = [gc_cols[:, n_heads + h:n_heads + h + 1] for h in heads]
    dec_incl = [jnp.exp(jnp.where(incl, gcc[h] - gc_rows[h:h + 1, :], -jnp.inf)) for h in heads]
    egc = [jnp.exp(g) for g in gcc]
    g_last = []
    g_last_col = []
    for h in heads:
        lasts = [gcc[h][(i + 1) * ch - 1:(i + 1) * ch, :] for i in range(n_sub)]
        col = lasts[n_sub - 1]
        for i in reversed(range(n_sub - 1)):
            col = jnp.where(row_chunk == i, lasts[i], col)
        g_last.append(lasts)
        g_last_col.append(col)
    k_dec = [k[h] * jnp.exp(g_last_col[h] - gcc[h]) for h in heads]

    qk_kk = [_mm(jnp.concatenate([q[h], k[h]], axis=0), k[h], NT_DIMS) for h in heads]
    qk = [qk_kk[h][:tile] * dec_incl[h] for h in heads]
    x = [-(beta[h] * qk_kk[h][tile:] * jnp.where(strict, dec_incl[h], 0.0)) for h in heads]
    t_inv = [eye + x[h] for h in heads]
    power = 2
    while power < ch:
        x = [_mm(xh, xh) for xh in x]
        t_inv = [t_inv[h] + _mm(t_inv[h], x[h]) for h in heads]
        power *= 2
    sol = [_mm(t_inv[h], jnp.concatenate([beta[h] * v[h], (beta[h] * egc[h]) * k[h]], axis=1)) for h in heads]
    u_v = [s[:, :HEAD_DIM] for s in sol]
    w = [s[:, HEAD_DIM:] for s in sol]
    q_dec = [q[h] * egc[h] for h in heads]

    state = [s_ref[h] for h in heads]
    u_parts = [[] for _ in heads]
    qs_parts = [[] for _ in heads]
    for i in range(n_sub):
        rows = slice(i * ch, (i + 1) * ch)
        wq_s = [_mm(jnp.concatenate([w[h][rows], q_dec[h][rows]], axis=0), state[h]) for h in heads]
        u = [u_v[h][rows] - wq_s[h][:ch] for h in heads]
        state = [jnp.exp(g_last[h][i]) * state[h] + _mm(k_dec[h][rows], u[h], TN_DIMS) for h in heads]
        for h in heads:
            u_parts[h].append(u[h])
            qs_parts[h].append(wq_s[h][ch:])
    for h in heads:
        s_ref[h] = state[h]
        o = jnp.concatenate(qs_parts[h], axis=0) + _mm(qk[h], jnp.concatenate(u_parts[h], axis=0))
        y_ref[:, sls[h]] = _gated_rmsnorm(o, z_ref[:, sls[h]], norm_ref[...])

    @pl.when(c == n_tiles - 1)
    def _():
        s_out_ref[...] = s_ref[...]


def _gdn_prompt(p3, ba3, bat3, conv_w, alog_l, dtb_l, alog_c, dtb_c, norm, *, qkv_col, z_col):
    b, t, _ = p3.shape
    n_heads = 8
    width = n_heads * HEAD_DIM
    tile = GDN_TILE
    n_tiles = t // tile
    halo_per_tile = tile // SUBLANES
    kern = functools.partial(_gdn_prompt_kernel, n_heads=n_heads, n_tiles=n_tiles)

    def rows(col):
        return pl.BlockSpec((None, tile, width), lambda bi, c: (bi, c, col))

    def halo(col):
        return pl.BlockSpec((None, SUBLANES, width),
                            lambda bi, c: (bi, jnp.maximum(c * halo_per_tile - 1, 0), col))

    def const(shape):
        return pl.BlockSpec(shape, lambda bi, c: tuple(0 for _ in shape))

    return pl.pallas_call(
        kern,
        grid=(b, n_tiles),
        in_specs=[
            rows(qkv_col), rows(qkv_col + 1), rows(qkv_col + 2),
            halo(qkv_col), halo(qkv_col + 1), halo(qkv_col + 2),
            rows(z_col),
            pl.BlockSpec((None, tile, HEAD_DIM), lambda bi, c: (bi, c, 0)),
            pl.BlockSpec((None, 2 * SUBLANES, tile), lambda bi, c: (bi * n_tiles + c, 0, 0)),
            const((CONV_W, 3 * width)),
            const((1, HEAD_DIM)), const((1, HEAD_DIM)),
            const((n_heads, 1)), const((n_heads, 1)),
            const((1, HEAD_DIM)),
        ],
        out_specs=[
            pl.BlockSpec((None, tile, width), lambda bi, c: (bi, c, 0)),
            pl.BlockSpec((None, n_heads, HEAD_DIM, HEAD_DIM), lambda bi, c: (bi, 0, 0, 0)),
        ],
        out_shape=[
            jax.ShapeDtypeStruct((b, t, width), F32),
            jax.ShapeDtypeStruct((b, n_heads, HEAD_DIM, HEAD_DIM), F32),
        ],
        scratch_shapes=[pltpu.VMEM((n_heads, HEAD_DIM, HEAD_DIM), F32)],
        compiler_params=_params("parallel", "arbitrary"),
        name="gdn_prompt",
    )(p3, p3, p3, p3, p3, p3, p3, ba3, bat3, conv_w, alog_l, dtb_l, alog_c, dtb_c, norm)


def _gdn_sample_kernel(qkv_ref, conv_ref, z_ref, ba_ref, cw_ref, alog_l_ref, dtb_l_ref, norm_ref, s_ref,
                       y_ref, s_out_ref, *, n_heads):
    width3 = qkv_ref.shape[1]
    width = width3 // 3
    rows = qkv_ref.shape[0]
    w = cw_ref[...]
    y = conv_ref[:, 0:width3] * w[0:1, :]
    for i in range(1, CONV_W - 1):
        y = y + conv_ref[:, i * width3:(i + 1) * width3] * w[i:i + 1, :]
    y = y + qkv_ref[...] * w[CONV_W - 1:CONV_W, :]
    c = _silu(y)

    ba = ba_ref[...]
    beta_cols = _sigmoid(ba)
    g_cols = -jnp.exp(alog_l_ref[...]) * _softplus(ba + dtb_l_ref[...])
    decay_cols = jnp.exp(g_cols)
    eye = (lax.broadcasted_iota(jnp.int32, (HEAD_DIM, HEAD_DIM), 0)
           == lax.broadcasted_iota(jnp.int32, (HEAD_DIM, HEAD_DIM), 1)).astype(BF16)

    for h in range(n_heads):
        q = _l2norm(c[:, h * HEAD_DIM:(h + 1) * HEAD_DIM]) * (HEAD_DIM ** -0.5)
        k = _l2norm(c[:, width + h * HEAD_DIM:width + (h + 1) * HEAD_DIM])
        v = c[:, 2 * width + h * HEAD_DIM:2 * width + (h + 1) * HEAD_DIM]
        q_t = _mm_exact_lhs(eye, q, NT_DIMS)
        k_t = _mm_exact_lhs(eye, k, NT_DIMS)
        for r in range(rows):
            s = decay_cols[r:r + 1, n_heads + h:n_heads + h + 1] * s_ref[r, h]
            k_col = k_t[:, r:r + 1]
            u = beta_cols[r:r + 1, h:h + 1] * (v[r:r + 1, :] - jnp.sum(k_col * s, axis=0, keepdims=True))
            s = s + k_col * u
            s_out_ref[r, h] = s
            o = jnp.sum(q_t[:, r:r + 1] * s, axis=0, keepdims=True)
            zr = z_ref[r:r + 1, h * HEAD_DIM:(h + 1) * HEAD_DIM]
            y_ref[r:r + 1, h * HEAD_DIM:(h + 1) * HEAD_DIM] = _gated_rmsnorm(o, zr, norm_ref[...])


def _gdn_sample(p2, conv_prev2, ba2, state, conv_w, alog_l, dtb_l, norm, *, qkv_col, z_col):
    db = p2.shape[0]
    n_heads = state.shape[1]
    width = n_heads * HEAD_DIM
    rows = SUBLANES
    kern = functools.partial(_gdn_sample_kernel, n_heads=n_heads)

    def const(shape):
        return pl.BlockSpec(shape, lambda i: tuple(0 for _ in shape))

    return pl.pallas_call(
        kern,
        grid=(db // rows,),
        in_specs=[
            pl.BlockSpec((rows, 3 * width), lambda i: (i, qkv_col // 3)),
            pl.BlockSpec((rows, (CONV_W - 1) * 3 * width), lambda i: (i, 0)),
            pl.BlockSpec((rows, width), lambda i: (i, z_col)),
            pl.BlockSpec((rows, HEAD_DIM), lambda i: (i, 0)),
            const((CONV_W, 3 * width)),
            const((1, HEAD_DIM)), const((1, HEAD_DIM)), const((1, HEAD_DIM)),
            pl.BlockSpec((rows, n_heads, HEAD_DIM, HEAD_DIM), lambda i: (i, 0, 0, 0)),
        ],
        out_specs=[
            pl.BlockSpec((rows, width), lambda i: (i, 0)),
            pl.BlockSpec((rows, n_heads, HEAD_DIM, HEAD_DIM), lambda i: (i, 0, 0, 0)),
        ],
        out_shape=[
            jax.ShapeDtypeStruct((db, width), F32),
            jax.ShapeDtypeStruct(state.shape, F32),
        ],
        compiler_params=_params("parallel"),
        name="gdn_sample",
    )(p2, conv_prev2, p2, ba2, conv_w, alog_l, dtb_l, norm, state)


def _merge_kernel(x_ref, yg_ref, ym_ref, sg_ref, sm_ref, wpg_ref, wpm_ref, wo_ref, o_ref):
    yg = jnp.dot(yg_ref[...].astype(BF16), wpg_ref[...], preferred_element_type=F32)
    ym = jnp.dot(ym_ref[...].astype(BF16), wpm_ref[...], preferred_element_type=F32)
    mixed = sg_ref[...] * yg + sm_ref[...] * ym
    o_ref[...] = x_ref[...] + jnp.dot(mixed.astype(BF16), wo_ref[...], preferred_element_type=F32)


def _merge(x, yg, ym, p, w_pg, w_pm, w_o, *, tm, sg_col, sm_col):
    m, d = x.shape

    def rows(col):
        return pl.BlockSpec((tm, d), lambda i: (i, col))

    def weight():
        return pl.BlockSpec((d, d), lambda i: (0, 0))

    return pl.pallas_call(
        _merge_kernel,
        grid=(m // tm,),
        in_specs=[rows(0), rows(0), rows(0), rows(sg_col), rows(sm_col), weight(), weight(), weight()],
        out_specs=rows(0),
        out_shape=jax.ShapeDtypeStruct((m, d), F32),
        compiler_params=_params("parallel"),
        name="merge",
    )(x, yg, ym, p, p, w_pg, w_pm, w_o)


def _ffn_kernel(x_ref, ln_ref, wg_ref, wu_ref, wd_ref, lnf_ref, o_ref, xn_ref, acc_ref, *, final):
    j = pl.program_id(1)

    @pl.when(j == 0)
    def _():
        x = x_ref[...]
        y = x * lax.rsqrt(jnp.mean(x * x, axis=-1, keepdims=True) + EPS)
        xn_ref[...] = (y * ln_ref[...]).astype(BF16)
        acc_ref[...] = jnp.zeros_like(acc_ref)

    xn = xn_ref[...]
    g = jnp.dot(xn, wg_ref[...], preferred_element_type=F32)
    u = jnp.dot(xn, wu_ref[...], preferred_element_type=F32)
    acc_ref[...] += jnp.dot((_silu(g) * u).astype(BF16), wd_ref[...], preferred_element_type=F32)

    @pl.when(j == pl.num_programs(1) - 1)
    def _():
        out = x_ref[...] + acc_ref[...]
        if final:
            out = out * lax.rsqrt(jnp.mean(out * out, axis=-1, keepdims=True) + EPS) * lnf_ref[...]
        o_ref[...] = out


def _ffn(x, ln, w_gu, w_dn, ln_final, *, tm, tf, final):
    m, d = x.shape
    d_ff = w_dn.shape[0]
    n_f = d_ff // tf
    return pl.pallas_call(
        functools.partial(_ffn_kernel, final=final),
        grid=(m // tm, n_f),
        in_specs=[
            pl.BlockSpec((tm, d), lambda i, j: (i, 0)),
            pl.BlockSpec((1, d), lambda i, j: (0, 0)),
            pl.BlockSpec((d, tf), lambda i, j: (0, j)),
            pl.BlockSpec((d, tf), lambda i, j: (0, n_f + j)),
            pl.BlockSpec((tf, d), lambda i, j: (j, 0)),
            pl.BlockSpec((1, d), lambda i, j: (0, 0)),
        ],
        out_specs=pl.BlockSpec((tm, d), lambda i, j: (i, 0)),
        out_shape=jax.ShapeDtypeStruct((m, d), F32),
        scratch_shapes=[pltpu.VMEM((tm, d), BF16), pltpu.VMEM((tm, d), F32)],
        compiler_params=_params("parallel", "arbitrary"),
        name="ffn",
    )(x, ln, w_gu, w_gu, w_dn, ln_final)


def _rope_tables(pos):
    half = HEAD_DIM // 2
    inv = jnp.exp(-math.log(ROPE_THETA) * jnp.arange(half, dtype=F32) / half)
    ang = pos.astype(F32)[:, None] * inv[None, :]
    cos = jnp.cos(ang)
    sin = jnp.sin(ang)
    return jnp.concatenate([cos, cos], axis=-1), jnp.concatenate([-sin, sin], axis=-1)


def _lane_row(vec, offset):
    return jnp.zeros((1, HEAD_DIM), F32).at[0, offset:offset + vec.shape[0]].set(vec.astype(F32))


def kernel(x_prompt, x_sample, cache_k, cache_v, state_ssm, state_conv, page_table, ln_mix, ln_ffn, w_in,
           conv_w, a_log, dt_bias, gdn_norm, w_branch_gdn, w_branch_moba, w_out, w_gate_up, w_down, ln_final):
    b, t, d = x_prompt.shape
    db, ts, _ = x_sample.shape
    depth = w_in.shape[0]
    n_heads = d // HEAD_DIM
    width = n_heads * HEAD_DIM
    n_pages = page_table.shape[1]
    past = n_pages * PAGE_SIZE
    d_ff = w_down.shape[1]
    assert ts == 1 and n_heads == 8 and width == d
    assert t % MOBA_BLOCK == 0 and t % 1024 == 0 and db % SUBLANES == 0

    c_ba = 7 * width
    c_g = c_ba + 2 * n_heads
    col_q, col_k, col_v, col_qkv, col_z, col_sg, col_sm = 0, 1, 2, 3, 6, 7, 8

    cos_p, sin_p = _rope_tables(jnp.arange(t, dtype=jnp.int32))
    cos_s, sin_s = _rope_tables(jnp.full((db,), past, dtype=jnp.int32))

    tm_p = 1024
    tf = d_ff // 2 if (d_ff // 2) % HEAD_DIM == 0 else d_ff
    xp = x_prompt.reshape(b * t, d)
    xs = x_sample.reshape(db * ts, d)
    outs = {name: [] for name in ("kp", "vp", "sp", "cp", "ks", "vs", "ss", "cs")}
    for l in range(depth):
        last = l == depth - 1
        w_main = jnp.concatenate([w_in[l, :, :c_ba], w_in[l, :, c_g:]], axis=1).astype(BF16)
        w_ba = jnp.pad(w_in[l, :, c_ba:c_g], ((0, 0), (0, HEAD_DIM - 2 * n_heads))).astype(BF16)
        w_bat = w_in[l, :, c_ba:c_g].T.astype(BF16)
        ln_m = ln_mix[l].reshape(1, d)
        ln_f = ln_ffn[l].reshape(1, d)
        alog_l = _lane_row(a_log[l], n_heads)
        dtb_l = _lane_row(dt_bias[l], n_heads)
        alog_c = a_log[l].reshape(n_heads, 1).astype(F32)
        dtb_c = dt_bias[l].reshape(n_heads, 1).astype(F32)
        norm = gdn_norm[l].reshape(1, HEAD_DIM)
        w_pg = w_branch_gdn[l].astype(BF16)
        w_pm = w_branch_moba[l].astype(BF16)
        w_o = w_out[l].astype(BF16)
        w_gu = w_gate_up[l].astype(BF16)
        w_dn = w_down[l].astype(BF16)
        proj = functools.partial(_in_proj, tn=width, n_rope_tiles=2, first_sigmoid_tile=col_sg)

        p, ba, bat = proj(xp, ln_m, w_main, w_ba, w_bat, cos_p, sin_p, tm=tm_p)
        p3 = p.reshape(b, t, -1)
        ym = _moba_prompt(p3, q_col=col_q * n_heads, k_col=col_k * n_heads, v_col=col_v * n_heads)
        bat3 = bat.reshape(2 * n_heads, b * t // GDN_TILE, GDN_TILE).transpose(1, 0, 2)
        yg, s_fin = _gdn_prompt(p3, ba.reshape(b, t, HEAD_DIM), bat3, conv_w[l], alog_l, dtb_l, alog_c, dtb_c,
                                norm, qkv_col=col_qkv, z_col=col_z)
        x1 = _merge(xp, yg.reshape(b * t, width), ym.reshape(b * t, width), p, w_pg, w_pm, w_o,
                    tm=tm_p // 2, sg_col=col_sg, sm_col=col_sm)
        xp = _ffn(x1, ln_f, w_gu, w_dn, ln_final.reshape(1, d), tm=tm_p, tf=tf, final=last)
        outs["kp"].append(p3[:, :, col_k * width:(col_k + 1) * width])
        outs["vp"].append(p3[:, :, col_v * width:(col_v + 1) * width])
        outs["sp"].append(s_fin)
        outs["cp"].append(p3[:, t - (CONV_W - 1):, col_qkv * width:(col_qkv + 3) * width])

        p, ba, bat = proj(xs, ln_m, w_main, w_ba, w_bat, cos_s, sin_s, tm=db)
        ph = p.reshape(db, -1, HEAD_DIM)
        ym = _moba_sample(ph, cache_k, cache_v, page_table, l, q_row=col_q, k_row=col_k, v_row=col_v)
        yg, s_new = _gdn_sample(p, state_conv[l].reshape(db, -1), ba, state_ssm[l], conv_w[l], alog_l, dtb_l,
                                norm, qkv_col=col_qkv, z_col=col_z)
        x1 = _merge(xs, yg, ym.reshape(db, width), p, w_pg, w_pm, w_o, tm=db, sg_col=col_sg, sm_col=col_sm)
        xs = _ffn(x1, ln_f, w_gu, w_dn, ln_final.reshape(1, d), tm=db, tf=tf, final=last)
        qkv_new = p[:, col_qkv * width:(col_qkv + 3) * width]
        outs["ks"].append(p[:, col_k * width:(col_k + 1) * width])
        outs["vs"].append(p[:, col_v * width:(col_v + 1) * width])
        outs["ss"].append(s_new)
        outs["cs"].append(jnp.concatenate([state_conv[l][:, 1:], qkv_new[:, None, :]], axis=1))

    n_tpages = t // PAGE_SIZE
    return (
        xp.reshape(b, t, d),
        xs.reshape(db, ts, d),
        jnp.stack(outs["kp"]).reshape(depth, b, n_tpages, PAGE_SIZE, n_heads, HEAD_DIM),
        jnp.stack(outs["vp"]).reshape(depth, b, n_tpages, PAGE_SIZE, n_heads, HEAD_DIM),
        jnp.stack(outs["sp"]),
        jnp.stack(outs["cp"]),
        jnp.stack(outs["ks"]).reshape(depth, db, ts, n_heads, HEAD_DIM),
        jnp.stack(outs["vs"]).reshape(depth, db, ts, n_heads, HEAD_DIM),
        jnp.stack(outs["ss"]),
        jnp.stack(outs["cs"]),
    )
```

```python
import functools
import math

import jax
import jax.numpy as jnp
import numpy as np
from jax import lax
from jax.experimental import pallas as pl
from jax.experimental.pallas import tpu as pltpu

F32 = jnp.float32
BF16 = jnp.bfloat16

HEAD_DIM = 128
MOBA_BLOCK = 256
MOBA_TOPK = 3
MOBA_PAIR_UNROLL = 8
GDN_CHUNK = 64
GDN_TILE = 2 * GDN_CHUNK
CONV_W = 4
PAGE_SIZE = 128
ROPE_THETA = 10000.0
EPS = 1e-6
SUBLANES = 8
VMEM_LIMIT_BYTES = 50 * 1024 * 1024
LOG2_E = 1.4426950408889634

NT_DIMS = (((1,), (1,)), ((), ()))
TN_DIMS = (((0,), (0,)), ((), ()))


def _params(*semantics):
    return pltpu.CompilerParams(dimension_semantics=semantics, vmem_limit_bytes=VMEM_LIMIT_BYTES)


def _mm(a, b, dims=None):
    a = a.astype(BF16)
    b = b.astype(BF16)
    if dims is None:
        return jnp.dot(a, b, preferred_element_type=F32)
    return lax.dot_general(a, b, dims, preferred_element_type=F32)


def _split3(x):
    hi = x.astype(BF16)
    r1 = x - hi.astype(F32)
    mid = r1.astype(BF16)
    lo = (r1 - mid.astype(F32)).astype(BF16)
    return hi, mid, lo


def _mm_exact_rhs(a, b_exact, dims=None):
    out = None
    for part in _split3(a):
        t = _mm(part, b_exact, dims)
        out = t if out is None else out + t
    return out


def _mm_exact_lhs(a_exact, b, dims=None):
    out = None
    for part in _split3(b):
        t = _mm(a_exact, part, dims)
        out = t if out is None else out + t
    return out


def _sigmoid(x):
    return 1.0 / (1.0 + jnp.exp(-x))


def _silu(x):
    return x * _sigmoid(x)


def _softplus(x):
    return jnp.maximum(x, 0.0) + jnp.log1p(jnp.exp(-jnp.abs(x)))


def _in_proj_kernel(x_ref, ln_ref, w_ref, wba_ref, wbat_ref, cos_ref, sin_ref,
                    p_ref, ba_ref, bat_ref, xn_ref, *, n_rope_tiles, first_sigmoid_tile):
    j = pl.program_id(1)

    @pl.when(j == 0)
    def _():
        x = x_ref[...]
        y = x * lax.rsqrt(jnp.mean(x * x, axis=-1, keepdims=True) + EPS)
        xn = (y * ln_ref[...]).astype(BF16)
        xn_ref[...] = xn
        ba_ref[...] = jnp.dot(xn, wba_ref[...], preferred_element_type=F32)
        bat_ref[...] = lax.dot_general(wbat_ref[...], xn, NT_DIMS, preferred_element_type=F32)

    acc = jnp.dot(xn_ref[...], w_ref[...], preferred_element_type=F32)

    @pl.when(j < n_rope_tiles)
    def _():
        cos = cos_ref[...]
        sin = sin_ref[...]
        for h in range(acc.shape[1] // HEAD_DIM):
            sl = slice(h * HEAD_DIM, (h + 1) * HEAD_DIM)
            xh = acc[:, sl]
            p_ref[:, sl] = xh * cos + pltpu.roll(xh, HEAD_DIM // 2, 1) * sin

    @pl.when((j >= n_rope_tiles) & (j < first_sigmoid_tile))
    def _():
        p_ref[...] = acc

    @pl.when(j >= first_sigmoid_tile)
    def _():
        p_ref[...] = _sigmoid(acc)


def _in_proj(x, ln, w_main, w_ba, w_bat, cos, sin, *, tm, tn, n_rope_tiles, first_sigmoid_tile):
    m, d = x.shape
    n = w_main.shape[1]
    rope_blocks = cos.shape[0] // tm
    kern = functools.partial(_in_proj_kernel, n_rope_tiles=n_rope_tiles,
                             first_sigmoid_tile=first_sigmoid_tile)
    return pl.pallas_call(
        kern,
        grid=(m // tm, n // tn),
        in_specs=[
            pl.BlockSpec((tm, d), lambda i, j: (i, 0)),
            pl.BlockSpec((1, d), lambda i, j: (0, 0)),
            pl.BlockSpec((d, tn), lambda i, j: (0, j)),
            pl.BlockSpec((d, HEAD_DIM), lambda i, j: (0, 0)),
            pl.BlockSpec((2 * SUBLANES, d), lambda i, j: (0, 0)),
            pl.BlockSpec((tm, HEAD_DIM), lambda i, j: (i % rope_blocks, 0)),
            pl.BlockSpec((tm, HEAD_DIM), lambda i, j: (i % rope_blocks, 0)),
        ],
        out_specs=[
            pl.BlockSpec((tm, tn), lambda i, j: (i, j)),
            pl.BlockSpec((tm, HEAD_DIM), lambda i, j: (i, 0)),
            pl.BlockSpec((2 * SUBLANES, tm), lambda i, j: (0, i)),
        ],
        out_shape=[
            jax.ShapeDtypeStruct((m, n), F32),
            jax.ShapeDtypeStruct((m, HEAD_DIM), F32),
            jax.ShapeDtypeStruct((2 * SUBLANES, m), F32),
        ],
        scratch_shapes=[pltpu.VMEM((tm, d), BF16)],
        compiler_params=_params("parallel", "arbitrary"),
        name="in_proj",
    )(x, ln, w_main, w_ba, w_bat, cos, sin)


def _top_blocks_cols(gate, n_valid, n_top):
    n, cols = gate.shape
    blk = lax.broadcasted_iota(jnp.int32, (n, cols), 0).astype(F32)
    g = jnp.where(blk < n_valid, gate, -jnp.inf)
    picks = []
    for _ in range(n_top):
        m = jnp.max(g, axis=0, keepdims=True)
        cand = jnp.where((g == m) & (m > -jnp.inf), blk, float(n))
        idx = jnp.min(cand, axis=0, keepdims=True)
        picks.append(idx)
        g = jnp.where(blk == idx, -jnp.inf, g)
    return picks


def _moba_prompt_kernel(pj_ref, pq_ref, q_ref, k_ref, v_ref, o_ref,
                        kb_ref, vt_ref, qt_ref, kmean_ref, picks_ref, m_ref, l_ref, acc_ref,
                        *, n_blocks, n_groups, unroll):
    blk = MOBA_BLOCK
    n_top = min(MOBA_TOPK, n_blocks)
    score_scale = (HEAD_DIM ** -0.5) * LOG2_E

    def block_rows(j):
        return pl.ds(pl.multiple_of(j * blk, blk), blk)

    def prepare_block(j, carry):
        kj = k_ref[block_rows(j), :]
        kb_ref[block_rows(j), :] = kj.astype(BF16)
        vt_ref[j] = v_ref[block_rows(j), :].T.astype(BF16)
        kmean_ref[pl.ds(j, 1), :] = jnp.mean(kj, axis=0, keepdims=True)
        return carry

    lax.fori_loop(0, n_blocks, prepare_block, 0)

    key = lax.broadcasted_iota(jnp.int32, (blk, blk), 0)
    qry = lax.broadcasted_iota(jnp.int32, (blk, blk), 1)
    no_pick = jnp.full((SUBLANES - n_top, blk), -1.0, F32)

    def init_tile(qb):
        q_t = q_ref[block_rows(qb), :].T
        gate = jnp.dot(kmean_ref[...], q_t, preferred_element_type=F32, precision=lax.Precision.HIGHEST)
        picks = _top_blocks_cols(gate, lax.convert_element_type(qb, F32), n_top)
        picks_ref[qb] = jnp.concatenate(picks + [no_pick], axis=0)
        q16 = (q_t * score_scale).astype(BF16)
        qt_ref[qb] = q16
        s = jnp.dot(kb_ref[block_rows(qb), :], q16, preferred_element_type=F32)
        s = jnp.where(key <= qry, s, -jnp.inf)
        m = jnp.max(s, axis=0, keepdims=True)
        p = jnp.exp2(s - m)
        m_ref[qb] = m
        l_ref[qb] = jnp.sum(p, axis=0, keepdims=True)
        acc_ref[qb] = jnp.dot(vt_ref[qb], p.astype(BF16), preferred_element_type=F32)

    def init_tiles(i, carry):
        for u in range(unroll):
            init_tile(i * unroll + u)
        return carry

    lax.fori_loop(0, n_blocks // unroll, init_tiles, 0)

    picks_ref[n_blocks] = jnp.full((SUBLANES, blk), -1.0, F32)
    qt_ref[n_blocks] = jnp.zeros((HEAD_DIM, blk), BF16)
    m_ref[n_blocks] = jnp.zeros((1, blk), F32)
    l_ref[n_blocks] = jnp.zeros((1, blk), F32)
    acc_ref[n_blocks] = jnp.zeros((HEAD_DIM, blk), F32)

    def pair_group(g, carry):
        lanes = range(unroll)
        js = [pj_ref[g * unroll + u] for u in lanes]
        qs = [pq_ref[g * unroll + u] for u in lanes]
        m_old = [m_ref[qs[u]] for u in lanes]
        l_old = [l_ref[qs[u]] for u in lanes]
        acc_old = [acc_ref[qs[u]] for u in lanes]
        picks = [picks_ref[qs[u]] for u in lanes]
        s = [jnp.dot(kb_ref[block_rows(js[u]), :], qt_ref[qs[u]], preferred_element_type=F32) for u in lanes]
        sel = []
        for u in lanes:
            jf = lax.convert_element_type(js[u], F32)
            hit = picks[u][0:1, :] == jf
            for t in range(1, n_top):
                hit = hit | (picks[u][t:t + 1, :] == jf)
            sel.append(hit)
        m_blk = [jnp.where(sel[u], jnp.max(s[u], axis=0, keepdims=True), -jnp.inf) for u in lanes]
        m_new = [jnp.maximum(m_old[u], m_blk[u]) for u in lanes]
        alpha = [jnp.exp2(m_old[u] - m_new[u]) for u in lanes]
        shift = [jnp.where(sel[u], m_new[u], jnp.inf) for u in lanes]
        p = [jnp.exp2(s[u] - shift[u]) for u in lanes]
        l_new = [alpha[u] * l_old[u] + jnp.sum(p[u], axis=0, keepdims=True) for u in lanes]
        acc_new = [alpha[u] * acc_old[u]
                   + jnp.dot(vt_ref[js[u]], p[u].astype(BF16), preferred_element_type=F32) for u in lanes]
        for u in lanes:
            m_ref[qs[u]] = m_new[u]
            l_ref[qs[u]] = l_new[u]
            acc_ref[qs[u]] = acc_new[u]
        return carry

    lax.fori_loop(0, n_groups, pair_group, 0)

    def finish_tile(qb, carry):
        o_ref[block_rows(qb), :] = (acc_ref[qb] / l_ref[qb]).T
        return carry

    lax.fori_loop(0, n_blocks, finish_tile, 0)


def _moba_pair_table(n_blocks, unroll):
    todo = {q: list(range(q)) for q in range(1, n_blocks)}
    pj, pq = [], []
    while todo:
        tiles = sorted(todo, key=lambda q: -len(todo[q]))[:unroll]
        for q in tiles:
            pj.append(todo[q].pop())
            pq.append(q)
            if not todo[q]:
                del todo[q]
        pj += [0] * (unroll - len(tiles))
        pq += [n_blocks] * (unroll - len(tiles))
    if not pj:
        pj, pq = [0] * unroll, [n_blocks] * unroll
    return np.asarray(pj, np.int32), np.asarray(pq, np.int32)


def _moba_prompt(p3, *, q_col, k_col, v_col):
    b, t, _ = p3.shape
    n_heads = 8
    nb = t // MOBA_BLOCK
    unroll = MOBA_PAIR_UNROLL if nb % MOBA_PAIR_UNROLL == 0 else 1
    pair_j, pair_q = _moba_pair_table(nb, unroll)
    kern = functools.partial(_moba_prompt_kernel, n_blocks=nb, n_groups=len(pair_j) // unroll, unroll=unroll)

    def seq(col):
        return pl.BlockSpec((None, t, HEAD_DIM), lambda bi, h, pj, pq: (bi, 0, col + h))

    grid_spec = pltpu.PrefetchScalarGridSpec(
        num_scalar_prefetch=2,
        grid=(b, n_heads),
        in_specs=[seq(q_col), seq(k_col), seq(v_col)],
        out_specs=seq(0),
        scratch_shapes=[
            pltpu.VMEM((t, HEAD_DIM), BF16),
            pltpu.VMEM((nb, HEAD_DIM, MOBA_BLOCK), BF16),
            pltpu.VMEM((nb + 1, HEAD_DIM, MOBA_BLOCK), BF16),
            pltpu.VMEM((nb, HEAD_DIM), F32),
            pltpu.VMEM((nb + 1, SUBLANES, MOBA_BLOCK), F32),
            pltpu.VMEM((nb + 1, 1, MOBA_BLOCK), F32),
            pltpu.VMEM((nb + 1, 1, MOBA_BLOCK), F32),
            pltpu.VMEM((nb + 1, HEAD_DIM, MOBA_BLOCK), F32),
        ],
    )
    return pl.pallas_call(
        kern,
        grid_spec=grid_spec,
        out_shape=jax.ShapeDtypeStruct((b, t, n_heads * HEAD_DIM), F32),
        compiler_params=_params("parallel", "parallel"),
        name="moba_prompt",
    )(jnp.asarray(pair_j), jnp.asarray(pair_q), p3, p3, p3)


def _moba_sample_kernel(pt_ref, q_ref, kn_ref, vn_ref, *refs, n_pages):
    del pt_ref
    k_refs = refs[:n_pages]
    v_refs = refs[n_pages:2 * n_pages]
    o_ref = refs[2 * n_pages]
    scale = HEAD_DIM ** -0.5
    pages_per_block = MOBA_BLOCK // PAGE_SIZE
    n_past = n_pages // pages_per_block
    q = q_ref[...]

    gates, ms, ls, accs = [], [], [], []
    for j in range(n_past):
        pages = range(j * pages_per_block, (j + 1) * pages_per_block)
        ks = [k_refs[pg][...] for pg in pages]
        ss = [jnp.sum(k * q[None], axis=-1, keepdims=True) * scale for k in ks]
        ksum = ks[0].sum(axis=0)
        m = jnp.max(ss[0], axis=0)
        for k, s in zip(ks[1:], ss[1:]):
            ksum = ksum + k.sum(axis=0)
            m = jnp.maximum(m, jnp.max(s, axis=0))
        l = None
        acc = None
        for pg, s in zip(pages, ss):
            p = jnp.exp(s - m[None])
            pl_sum = p.sum(axis=0)
            pv = (p * v_refs[pg][...]).sum(axis=0)
            l = pl_sum if l is None else l + pl_sum
            acc = pv if acc is None else acc + pv
        gates.append(jnp.sum(q * (ksum * (1.0 / MOBA_BLOCK)), axis=-1, keepdims=True))
        ms.append(m)
        ls.append(l)
        accs.append(acc)

    n_top = min(MOBA_TOPK, n_past + 1)
    remaining = list(gates)
    selected = [jnp.zeros_like(gates[0], dtype=jnp.bool_) for _ in range(n_past)]
    for _ in range(n_top):
        best = remaining[0]
        for g in remaining[1:]:
            best = jnp.maximum(best, g)
        idx = jnp.full_like(best, float(n_past))
        for j in reversed(range(n_past)):
            idx = jnp.where(remaining[j] == best, float(j), idx)
        idx = jnp.where(best > -jnp.inf, idx, float(n_past))
        for j in range(n_past):
            hit = idx == float(j)
            selected[j] = selected[j] | hit
            remaining[j] = jnp.where(hit, -jnp.inf, remaining[j])

    s_own = jnp.sum(q * kn_ref[...], axis=-1, keepdims=True) * scale
    m_tot = s_own
    for j in range(n_past):
        m_tot = jnp.maximum(m_tot, jnp.where(selected[j], ms[j], -jnp.inf))
    e_own = jnp.exp(s_own - m_tot)
    l_tot = e_own
    o = e_own * vn_ref[...]
    for j in range(n_past):
        w = jnp.where(selected[j], jnp.exp(ms[j] - m_tot), 0.0)
        l_tot = l_tot + w * ls[j]
        o = o + w * accs[j]
    o_ref[...] = o / l_tot


def _moba_sample(ph, cache_k, cache_v, page_table, layer, *, q_row, k_row, v_row):
    db = ph.shape[0]
    n_pages = page_table.shape[1]
    n_heads = cache_k.shape[3]
    assert MOBA_BLOCK % PAGE_SIZE == 0 and n_pages % (MOBA_BLOCK // PAGE_SIZE) == 0
    assert n_heads == SUBLANES
    kern = functools.partial(_moba_sample_kernel, n_pages=n_pages)

    def row_spec(row):
        return pl.BlockSpec((None, n_heads, HEAD_DIM), lambda bi, pt: (bi, row, 0))

    def page_spec(page):
        return pl.BlockSpec((None, None, PAGE_SIZE, n_heads, HEAD_DIM),
                            lambda bi, pt: (layer, pt[bi * n_pages + page], 0, 0, 0))

    grid_spec = pltpu.PrefetchScalarGridSpec(
        num_scalar_prefetch=1,
        grid=(db,),
        in_specs=([row_spec(q_row), row_spec(k_row), row_spec(v_row)]
                  + [page_spec(pg) for pg in range(n_pages)] * 2),
        out_specs=pl.BlockSpec((None, n_heads, HEAD_DIM), lambda bi, pt: (bi, 0, 0)),
    )
    return pl.pallas_call(
        kern,
        grid_spec=grid_spec,
        out_shape=jax.ShapeDtypeStruct((db, n_heads, HEAD_DIM), F32),
        compiler_params=_params("parallel"),
        name="moba_sample",
    )(page_table.reshape(-1), ph, ph, ph, *([cache_k] * n_pages), *([cache_v] * n_pages))


def _l2norm(x):
    return x * lax.rsqrt(jnp.sum(x * x, axis=-1, keepdims=True) + 1e-6)


def _gated_rmsnorm(o, z, w):
    on = o * lax.rsqrt(jnp.mean(o * o, axis=-1, keepdims=True) + EPS) * w
    return on * _silu(z)


def _gdn_prompt_kernel(q_ref, k_ref, v_ref, qh_ref, kh_ref, vh_ref, z_ref, bat_ref,
                       cw_ref, alog_c_ref, dtb_c_ref, norm_ref,
                       y_ref, s_out_ref, s_ref, *, n_heads, n_tiles):
    c = pl.program_id(1)
    ch = GDN_CHUNK
    tile = GDN_TILE
    n_sub = tile // ch
    heads = range(n_heads)

    @pl.when(c == 0)
    def _():
        s_ref[...] = jnp.zeros_like(s_ref)

    def conv(u_ref, halo_ref, part):
        halo = jnp.where(c == 0, 0.0, halo_ref[...])
        ext = jnp.concatenate([halo, u_ref[...]], axis=0)
        w = cw_ref[:, part * n_heads * HEAD_DIM:(part + 1) * n_heads * HEAD_DIM]
        y = None
        for i in range(CONV_W):
            shift = CONV_W - 1 - i
            tap = ext if shift == 0 else pltpu.roll(ext, shift, 0)
            term = tap[SUBLANES:, :] * w[i:i + 1, :]
            y = term if y is None else y + term
        return _silu(y)

    cq = conv(q_ref, qh_ref, 0)
    ck = conv(k_ref, kh_ref, 1)
    cv = conv(v_ref, vh_ref, 2)

    ri = lax.broadcasted_iota(jnp.int32, (tile, tile), 0)
    ci = lax.broadcasted_iota(jnp.int32, (tile, tile), 1)
    chunk_shift = ch.bit_length() - 1
    same_chunk = jnp.right_shift(ri, chunk_shift) == jnp.right_shift(ci, chunk_shift)
    incl = same_chunk & (ri >= ci)
    strict = same_chunk & (ri > ci)
    eye = (ri == ci).astype(F32)
    row_chunk = jnp.right_shift(lax.broadcasted_iota(jnp.int32, (tile, 1), 0), chunk_shift)

    beta_rows = _sigmoid(bat_ref[:n_heads, :])
    g_rows = -jnp.exp(alog_c_ref[...]) * _softplus(bat_ref[n_heads:, :] + dtb_c_ref[...])
    gc_rows = _mm_exact_rhs(g_rows, (same_chunk & (ri <= ci)).astype(BF16))
    cols = _mm_exact_lhs(eye.astype(BF16), jnp.concatenate([beta_rows, gc_rows], axis=0), NT_DIMS)

    sls = [slice(h * HEAD_DIM, (h + 1) * HEAD_DIM) for h in heads]
    q = [_l2norm(cq[:, sl]) * (HEAD_DIM ** -0.5) for sl in sls]
    k = [_l2norm(ck[:, sl]) for sl in sls]
    v = [cv[:, sl] for sl in sls]
    beta = [cols[:, h:h + 1] for h in heads]
    gcc = [cols[:, n_heads + h:n_heads + h + 1] for h in heads]
    dec_incl = [jnp.exp(jnp.where(incl, gcc[h] - gc_rows[h:h + 1, :], -jnp.inf)) for h in heads]
    egc = [jnp.exp(g) for g in gcc]
    g_last = []
    g_last_col = []
    for h in heads:
        lasts = [gcc[h][(i + 1) * ch - 1:(i + 1) * ch, :] for i in range(n_sub)]
        col = lasts[n_sub - 1]
        for i in reversed(range(n_sub - 1)):
            col = jnp.where(row_chunk == i, lasts[i], col)
        g_last.append(lasts)
        g_last_col.append(col)
    k_dec = [k[h] * jnp.exp(g_last_col[h] - gcc[h]) for h in heads]

    qk_kk = [_mm(jnp.concatenate([q[h], k[h]], axis=0), k[h], NT_DIMS) for h in heads]
    qk = [qk_kk[h][:tile] * dec_incl[h] for h in heads]
    x = [-(beta[h] * qk_kk[h][tile:] * jnp.where(strict, dec_incl[h], 0.0)) for h in heads]
    t_inv = [eye + x[h] for h in heads]
    power = 2
    while power < ch:
        x = [_mm(xh, xh) for xh in x]
        t_inv = [t_inv[h] + _mm(t_inv[h], x[h]) for h in heads]
        power *= 2
    sol = [_mm(t_inv[h], jnp.concatenate([beta[h] * v[h], (beta[h] * egc[h]) * k[h]], axis=1)) for h in heads]
    u_v = [s[:, :HEAD_DIM] for s in sol]
    w = [s[:, HEAD_DIM:] for s in sol]
    q_dec = [q[h] * egc[h] for h in heads]

    state = [s_ref[h] for h in heads]
    u_parts = [[] for _ in heads]
    qs_parts = [[] for _ in heads]
    for i in range(n_sub):
        rows = slice(i * ch, (i + 1) * ch)
        wq_s = [_mm(jnp.concatenate([w[h][rows], q_dec[h][rows]], axis=0), state[h]) for h in heads]
        u = [u_v[h][rows] - wq_s[h][:ch] for h in heads]
        state = [jnp.exp(g_last[h][i]) * state[h] + _mm(k_dec[h][rows], u[h], TN_DIMS) for h in heads]
        for h in heads:
            u_parts[h].append(u[h])
            qs_parts[h].append(wq_s[h][ch:])
    for h in heads:
        s_ref[h] = state[h]
        o = jnp.concatenate(qs_parts[h], axis=0) + _mm(qk[h], jnp.concatenate(u_parts[h], axis=0))
        y_ref[:, sls[h]] = _gated_rmsnorm(o, z_ref[:, sls[h]], norm_ref[...])

    @pl.when(c == n_tiles - 1)
    def _():
        s_out_ref[...] = s_ref[...]


def _gdn_prompt(p3, bat3, conv_w, alog_c, dtb_c, norm, *, qkv_col, z_col):
    b, t, _ = p3.shape
    n_heads = 8
    width = n_heads * HEAD_DIM
    tile = GDN_TILE
    n_tiles = t // tile
    halo_per_tile = tile // SUBLANES
    kern = functools.partial(_gdn_prompt_kernel, n_heads=n_heads, n_tiles=n_tiles)

    def rows(col):
        return pl.BlockSpec((None, tile, width), lambda bi, c: (bi, c, col))

    def halo(col):
        return pl.BlockSpec((None, SUBLANES, width),
                            lambda bi, c: (bi, jnp.maximum(c * halo_per_tile - 1, 0), col))

    def const(shape):
        return pl.BlockSpec(shape, lambda bi, c: tuple(0 for _ in shape))

    return pl.pallas_call(
        kern,
        grid=(b, n_tiles),
        in_specs=[
            rows(qkv_col), rows(qkv_col + 1), rows(qkv_col + 2),
            halo(qkv_col), halo(qkv_col + 1), halo(qkv_col + 2),
            rows(z_col),
            pl.BlockSpec((None, 2 * SUBLANES, tile), lambda bi, c: (bi * n_tiles + c, 0, 0)),
            const((CONV_W, 3 * width)),
            const((n_heads, 1)), const((n_heads, 1)),
            const((1, HEAD_DIM)),
        ],
        out_specs=[
            pl.BlockSpec((None, tile, width), lambda bi, c: (bi, c, 0)),
            pl.BlockSpec((None, n_heads, HEAD_DIM, HEAD_DIM), lambda bi, c: (bi, 0, 0, 0)),
        ],
        out_shape=[
            jax.ShapeDtypeStruct((b, t, width), F32),
            jax.ShapeDtypeStruct((b, n_heads, HEAD_DIM, HEAD_DIM), F32),
        ],
        scratch_shapes=[pltpu.VMEM((n_heads, HEAD_DIM, HEAD_DIM), F32)],
        compiler_params=_params("parallel", "arbitrary"),
        name="gdn_prompt",
    )(p3, p3, p3, p3, p3, p3, p3, bat3, conv_w, alog_c, dtb_c, norm)


def _gdn_sample_kernel(qkv_ref, conv_ref, z_ref, ba_ref, cw_ref, alog_l_ref, dtb_l_ref, norm_ref, s_ref,
                       y_ref, s_out_ref, *, n_heads):
    width3 = qkv_ref.shape[1]
    width = width3 // 3
    rows = qkv_ref.shape[0]
    w = cw_ref[...]
    y = conv_ref[:, 0:width3] * w[0:1, :]
    for i in range(1, CONV_W - 1):
        y = y + conv_ref[:, i * width3:(i + 1) * width3] * w[i:i + 1, :]
    y = y + qkv_ref[...] * w[CONV_W - 1:CONV_W, :]
    c = _silu(y)

    ba = ba_ref[...]
    beta_cols = _sigmoid(ba)
    g_cols = -jnp.exp(alog_l_ref[...]) * _softplus(ba + dtb_l_ref[...])
    decay_cols = jnp.exp(g_cols)
    eye = (lax.broadcasted_iota(jnp.int32, (HEAD_DIM, HEAD_DIM), 0)
           == lax.broadcasted_iota(jnp.int32, (HEAD_DIM, HEAD_DIM), 1)).astype(BF16)

    for h in range(n_heads):
        q = _l2norm(c[:, h * HEAD_DIM:(h + 1) * HEAD_DIM]) * (HEAD_DIM ** -0.5)
        k = _l2norm(c[:, width + h * HEAD_DIM:width + (h + 1) * HEAD_DIM])
        v = c[:, 2 * width + h * HEAD_DIM:2 * width + (h + 1) * HEAD_DIM]
        q_t = _mm_exact_lhs(eye, q, NT_DIMS)
        k_t = _mm_exact_lhs(eye, k, NT_DIMS)
        for r in range(rows):
            s = decay_cols[r:r + 1, n_heads + h:n_heads + h + 1] * s_ref[r, h]
            k_col = k_t[:, r:r + 1]
            u = beta_cols[r:r + 1, h:h + 1] * (v[r:r + 1, :] - jnp.sum(k_col * s, axis=0, keepdims=True))
            s = s + k_col * u
            s_out_ref[r, h] = s
            o = jnp.sum(q_t[:, r:r + 1] * s, axis=0, keepdims=True)
            zr = z_ref[r:r + 1, h * HEAD_DIM:(h + 1) * HEAD_DIM]
            y_ref[r:r + 1, h * HEAD_DIM:(h + 1) * HEAD_DIM] = _gated_rmsnorm(o, zr, norm_ref[...])


def _gdn_sample(p2, conv_prev2, ba2, states, layer, conv_w, alog_l, dtb_l, norm, *, qkv_col, z_col):
    db = p2.shape[0]
    n_heads = states.shape[2]
    width = n_heads * HEAD_DIM
    rows = SUBLANES
    kern = functools.partial(_gdn_sample_kernel, n_heads=n_heads)

    def const(shape):
        return pl.BlockSpec(shape, lambda i: tuple(0 for _ in shape))

    return pl.pallas_call(
        kern,
        grid=(db // rows,),
        in_specs=[
            pl.BlockSpec((rows, 3 * width), lambda i: (i, qkv_col // 3)),
            pl.BlockSpec((rows, (CONV_W - 1) * 3 * width), lambda i: (i, 0)),
            pl.BlockSpec((rows, width), lambda i: (i, z_col)),
            pl.BlockSpec((rows, HEAD_DIM), lambda i: (i, 0)),
            const((CONV_W, 3 * width)),
            const((1, HEAD_DIM)), const((1, HEAD_DIM)), const((1, HEAD_DIM)),
            pl.BlockSpec((None, rows, n_heads, HEAD_DIM, HEAD_DIM), lambda i: (layer, i, 0, 0, 0)),
        ],
        out_specs=[
            pl.BlockSpec((rows, width), lambda i: (i, 0)),
            pl.BlockSpec((rows, n_heads, HEAD_DIM, HEAD_DIM), lambda i: (i, 0, 0, 0)),
        ],
        out_shape=[
            jax.ShapeDtypeStruct((db, width), F32),
            jax.ShapeDtypeStruct(states.shape[1:], F32),
        ],
        compiler_params=_params("parallel"),
        name="gdn_sample",
    )(p2, conv_prev2, p2, ba2, conv_w, alog_l, dtb_l, norm, states)


def _merge_kernel(x_ref, yg_ref, ym_ref, sg_ref, sm_ref, wpg_ref, wpm_ref, wo_ref, o_ref):
    yg = jnp.dot(yg_ref[...].astype(BF16), wpg_ref[...], preferred_element_type=F32)
    ym = jnp.dot(ym_ref[...].astype(BF16), wpm_ref[...], preferred_element_type=F32)
    mixed = sg_ref[...] * yg + sm_ref[...] * ym
    o_ref[...] = x_ref[...] + jnp.dot(mixed.astype(BF16), wo_ref[...], preferred_element_type=F32)


def _merge(x, yg, ym, p, w_pg, w_pm, w_o, *, tm, sg_col, sm_col):
    m, d = x.shape

    def rows(col):
        return pl.BlockSpec((tm, d), lambda i: (i, col))

    def weight():
        return pl.BlockSpec((d, d), lambda i: (0, 0))

    return pl.pallas_call(
        _merge_kernel,
        grid=(m // tm,),
        in_specs=[rows(0), rows(0), rows(0), rows(sg_col), rows(sm_col), weight(), weight(), weight()],
        out_specs=rows(0),
        out_shape=jax.ShapeDtypeStruct((m, d), F32),
        compiler_params=_params("parallel"),
        name="merge",
    )(x, yg, ym, p, p, w_pg, w_pm, w_o)


def _ffn_kernel(x_ref, ln_ref, wg_ref, wu_ref, wd_ref, lnf_ref, o_ref, xn_ref, acc_ref, *, final):
    j = pl.program_id(1)

    @pl.when(j == 0)
    def _():
        x = x_ref[...]
        y = x * lax.rsqrt(jnp.mean(x * x, axis=-1, keepdims=True) + EPS)
        xn_ref[...] = (y * ln_ref[...]).astype(BF16)
        acc_ref[...] = jnp.zeros_like(acc_ref)

    xn = xn_ref[...]
    g = jnp.dot(xn, wg_ref[...], preferred_element_type=F32)
    u = jnp.dot(xn, wu_ref[...], preferred_element_type=F32)
    acc_ref[...] += jnp.dot((_silu(g) * u).astype(BF16), wd_ref[...], preferred_element_type=F32)

    @pl.when(j == pl.num_programs(1) - 1)
    def _():
        out = x_ref[...] + acc_ref[...]
        if final:
            out = out * lax.rsqrt(jnp.mean(out * out, axis=-1, keepdims=True) + EPS) * lnf_ref[...]
        o_ref[...] = out


def _ffn(x, ln, w_gu, w_dn, ln_final, *, tm, tf, final):
    m, d = x.shape
    d_ff = w_dn.shape[0]
    n_f = d_ff // tf
    return pl.pallas_call(
        functools.partial(_ffn_kernel, final=final),
        grid=(m // tm, n_f),
        in_specs=[
            pl.BlockSpec((tm, d), lambda i, j: (i, 0)),
            pl.BlockSpec((1, d), lambda i, j: (0, 0)),
            pl.BlockSpec((d, tf), lambda i, j: (0, j)),
            pl.BlockSpec((d, tf), lambda i, j: (0, n_f + j)),
            pl.BlockSpec((tf, d), lambda i, j: (j, 0)),
            pl.BlockSpec((1, d), lambda i, j: (0, 0)),
        ],
        out_specs=pl.BlockSpec((tm, d), lambda i, j: (i, 0)),
        out_shape=jax.ShapeDtypeStruct((m, d), F32),
        scratch_shapes=[pltpu.VMEM((tm, d), BF16), pltpu.VMEM((tm, d), F32)],
        compiler_params=_params("parallel", "arbitrary"),
        name="ffn",
    )(x, ln, w_gu, w_gu, w_dn, ln_final)


def _rope_tables(pos):
    half = HEAD_DIM // 2
    inv = jnp.exp(-math.log(ROPE_THETA) * jnp.arange(half, dtype=F32) / half)
    ang = pos.astype(F32)[:, None] * inv[None, :]
    cos = jnp.cos(ang)
    sin = jnp.sin(ang)
    return jnp.concatenate([cos, cos], axis=-1), jnp.concatenate([-sin, sin], axis=-1)


def _lane_row(vec, offset):
    return jnp.zeros((1, HEAD_DIM), F32).at[0, offset:offset + vec.shape[0]].set(vec.astype(F32))


def kernel(x_prompt, x_sample, cache_k, cache_v, state_ssm, state_conv, page_table, ln_mix, ln_ffn, w_in,
           conv_w, a_log, dt_bias, gdn_norm, w_branch_gdn, w_branch_moba, w_out, w_gate_up, w_down, ln_final):
    b, t, d = x_prompt.shape
    db, ts, _ = x_sample.shape
    depth = w_in.shape[0]
    n_heads = d // HEAD_DIM
    width = n_heads * HEAD_DIM
    n_pages = page_table.shape[1]
    past = n_pages * PAGE_SIZE
    d_ff = w_down.shape[1]
    assert ts == 1 and n_heads == 8 and width == d
    assert t % MOBA_BLOCK == 0 and t % 1024 == 0 and db % SUBLANES == 0

    c_ba = 7 * width
    c_g = c_ba + 2 * n_heads
    col_q, col_k, col_v, col_qkv, col_z, col_sg, col_sm = 0, 1, 2, 3, 6, 7, 8

    cos_p, sin_p = _rope_tables(jnp.arange(t, dtype=jnp.int32))
    cos_s, sin_s = _rope_tables(jnp.full((db,), past, dtype=jnp.int32))

    tm_p = 1024
    tf = d_ff // 2 if (d_ff // 2) % HEAD_DIM == 0 else d_ff
    xp = x_prompt.reshape(b * t, d)
    xs = x_sample.reshape(db * ts, d)
    outs = {name: [] for name in ("kp", "vp", "sp", "cp", "ks", "vs", "ss", "cs")}
    for l in range(depth):
        last = l == depth - 1
        w_main = jnp.concatenate([w_in[l, :, :c_ba], w_in[l, :, c_g:]], axis=1).astype(BF16)
        w_ba = jnp.pad(w_in[l, :, c_ba:c_g], ((0, 0), (0, HEAD_DIM - 2 * n_heads))).astype(BF16)
        w_bat = w_in[l, :, c_ba:c_g].T.astype(BF16)
        ln_m = ln_mix[l].reshape(1, d)
        ln_f = ln_ffn[l].reshape(1, d)
        alog_l = _lane_row(a_log[l], n_heads)
        dtb_l = _lane_row(dt_bias[l], n_heads)
        alog_c = a_log[l].reshape(n_heads, 1).astype(F32)
        dtb_c = dt_bias[l].reshape(n_heads, 1).astype(F32)
        norm = gdn_norm[l].reshape(1, HEAD_DIM)
        w_pg = w_branch_gdn[l].astype(BF16)
        w_pm = w_branch_moba[l].astype(BF16)
        w_o = w_out[l].astype(BF16)
        w_gu = w_gate_up[l].astype(BF16)
        w_dn = w_down[l].astype(BF16)
        proj = functools.partial(_in_proj, tn=width, n_rope_tiles=2, first_sigmoid_tile=col_sg)

        p, ba, bat = proj(xp, ln_m, w_main, w_ba, w_bat, cos_p, sin_p, tm=tm_p)
        p3 = p.reshape(b, t, -1)
        ym = _moba_prompt(p3, q_col=col_q * n_heads, k_col=col_k * n_heads, v_col=col_v * n_heads)
        bat3 = bat.reshape(2 * n_heads, b * t // GDN_TILE, GDN_TILE).transpose(1, 0, 2)
        yg, s_fin = _gdn_prompt(p3, bat3, conv_w[l], alog_c, dtb_c, norm, qkv_col=col_qkv, z_col=col_z)
        x1 = _merge(xp, yg.reshape(b * t, width), ym.reshape(b * t, width), p, w_pg, w_pm, w_o,
                    tm=tm_p // 2, sg_col=col_sg, sm_col=col_sm)
        xp = _ffn(x1, ln_f, w_gu, w_dn, ln_final.reshape(1, d), tm=tm_p, tf=tf, final=last)
        outs["kp"].append(p3[:, :, col_k * width:(col_k + 1) * width])
        outs["vp"].append(p3[:, :, col_v * width:(col_v + 1) * width])
        outs["sp"].append(s_fin)
        outs["cp"].append(p3[:, t - (CONV_W - 1):, col_qkv * width:(col_qkv + 3) * width])

        p, ba, bat = proj(xs, ln_m, w_main, w_ba, w_bat, cos_s, sin_s, tm=db)
        ph = p.reshape(db, -1, HEAD_DIM)
        ym = _moba_sample(ph, cache_k, cache_v, page_table, l, q_row=col_q, k_row=col_k, v_row=col_v)
        yg, s_new = _gdn_sample(p, state_conv[l].reshape(db, -1), ba, state_ssm, l, conv_w[l], alog_l, dtb_l,
                                norm, qkv_col=col_qkv, z_col=col_z)
        x1 = _merge(xs, yg, ym.reshape(db, width), p, w_pg, w_pm, w_o, tm=db, sg_col=col_sg, sm_col=col_sm)
        xs = _ffn(x1, ln_f, w_gu, w_dn, ln_final.reshape(1, d), tm=db, tf=tf, final=last)
        qkv_new = p[:, col_qkv * width:(col_qkv + 3) * width]
        outs["ks"].append(p[:, col_k * width:(col_k + 1) * width])
        outs["vs"].append(p[:, col_v * width:(col_v + 1) * width])
        outs["ss"].append(s_new)
        outs["cs"].append(jnp.concatenate([state_conv[l][:, 1:], qkv_new[:, None, :]], axis=1))

    n_tpages = t // PAGE_SIZE
    return (
        xp.reshape(b, t, d),
        xs.reshape(db, ts, d),
        jnp.stack(outs["kp"]).reshape(depth, b, n_tpages, PAGE_SIZE, n_heads, HEAD_DIM),
        jnp.stack(outs["vp"]).reshape(depth, b, n_tpages, PAGE_SIZE, n_heads, HEAD_DIM),
        jnp.stack(outs["sp"]),
        jnp.stack(outs["cp"]),
        jnp.stack(outs["ks"]).reshape(depth, db, ts, n_heads, HEAD_DIM),
        jnp.stack(outs["vs"]).reshape(depth, db, ts, n_heads, HEAD_DIM),
        jnp.stack(outs["ss"]),
        jnp.stack(outs["cs"]),
    )
```

```python
import functools
import math

import jax
import jax.numpy as jnp
import numpy as np
from jax import lax
from jax.experimental import pallas as pl
from jax.experimental.pallas import tpu as pltpu

F32 = jnp.float32
BF16 = jnp.bfloat16

HEAD_DIM = 128
MOBA_BLOCK = 256
MOBA_TOPK = 3
MOBA_PAIR_UNROLL = 8
GDN_CHUNK = 64
GDN_TILE = 2 * GDN_CHUNK
CONV_W = 4
PAGE_SIZE = 128
ROPE_THETA = 10000.0
EPS = 1e-6
SUBLANES = 8
VMEM_LIMIT_BYTES = 50 * 1024 * 1024
LOG2_E = 1.4426950408889634

NT_DIMS = (((1,), (1,)), ((), ()))
TN_DIMS = (((0,), (0,)), ((), ()))


def _params(*semantics):
    return pltpu.CompilerParams(dimension_semantics=semantics, vmem_limit_bytes=VMEM_LIMIT_BYTES)


def _mm(a, b, dims=None):
    a = a.astype(BF16)
    b = b.astype(BF16)
    if dims is None:
        return jnp.dot(a, b, preferred_element_type=F32)
    return lax.dot_general(a, b, dims, preferred_element_type=F32)


def _split3(x):
    hi = x.astype(BF16)
    r1 = x - hi.astype(F32)
    mid = r1.astype(BF16)
    lo = (r1 - mid.astype(F32)).astype(BF16)
    return hi, mid, lo


def _mm_exact_rhs(a, b_exact, dims=None):
    out = None
    for part in _split3(a):
        t = _mm(part, b_exact, dims)
        out = t if out is None else out + t
    return out


def _mm_exact_lhs(a_exact, b, dims=None):
    out = None
    for part in _split3(b):
        t = _mm(a_exact, part, dims)
        out = t if out is None else out + t
    return out


def _sigmoid(x):
    return 1.0 / (1.0 + jnp.exp(-x))


def _silu(x):
    return x * _sigmoid(x)


def _softplus(x):
    return jnp.maximum(x, 0.0) + jnp.log1p(jnp.exp(-jnp.abs(x)))


def _in_proj_kernel(x_ref, ln_ref, wa_ref, wg_ref, wba_ref, wbat_ref, cos_ref, sin_ref,
                    p_ref, ba_ref, bat_ref, xn_ref, *, n_rope_tiles, n_lead_tiles):
    j = pl.program_id(1)

    @pl.when(j == 0)
    def _():
        x = x_ref[...]
        y = x * lax.rsqrt(jnp.mean(x * x, axis=-1, keepdims=True) + EPS)
        xn = (y * ln_ref[...]).astype(BF16)
        xn_ref[...] = xn
        ba_ref[...] = jnp.dot(xn, wba_ref[...], preferred_element_type=F32)
        bat_ref[...] = lax.dot_general(wbat_ref[...], xn, NT_DIMS, preferred_element_type=F32)

    @pl.when(j < n_rope_tiles)
    def _():
        acc = jnp.dot(xn_ref[...], wa_ref[...], preferred_element_type=F32)
        cos = cos_ref[...]
        sin = sin_ref[...]
        for h in range(acc.shape[1] // HEAD_DIM):
            sl = slice(h * HEAD_DIM, (h + 1) * HEAD_DIM)
            xh = acc[:, sl]
            p_ref[:, sl] = xh * cos + pltpu.roll(xh, HEAD_DIM // 2, 1) * sin

    @pl.when((j >= n_rope_tiles) & (j < n_lead_tiles))
    def _():
        p_ref[...] = jnp.dot(xn_ref[...], wa_ref[...], preferred_element_type=F32)

    @pl.when(j >= n_lead_tiles)
    def _():
        p_ref[...] = _sigmoid(jnp.dot(xn_ref[...], wg_ref[...], preferred_element_type=F32))


def _in_proj(x, ln, w_all, w_gates, w_ba, w_bat, cos, sin, layer, *, tm, tn, n_rope_tiles, n_lead_tiles):
    m, d = x.shape
    n_tiles = n_lead_tiles + w_gates.shape[2] // tn
    rope_blocks = cos.shape[0] // tm
    kern = functools.partial(_in_proj_kernel, n_rope_tiles=n_rope_tiles, n_lead_tiles=n_lead_tiles)
    n = n_tiles * tn
    return pl.pallas_call(
        kern,
        grid=(m // tm, n_tiles),
        in_specs=[
            pl.BlockSpec((tm, d), lambda i, j: (i, 0)),
            pl.BlockSpec((1, d), lambda i, j: (0, 0)),
            pl.BlockSpec((None, d, tn), lambda i, j: (layer, 0, jnp.minimum(j, n_lead_tiles - 1))),
            pl.BlockSpec((None, d, tn), lambda i, j: (layer, 0, jnp.maximum(j - n_lead_tiles, 0))),
            pl.BlockSpec((None, d, HEAD_DIM), lambda i, j: (layer, 0, 0)),
            pl.BlockSpec((None, 2 * SUBLANES, d), lambda i, j: (layer, 0, 0)),
            pl.BlockSpec((tm, HEAD_DIM), lambda i, j: (i % rope_blocks, 0)),
            pl.BlockSpec((tm, HEAD_DIM), lambda i, j: (i % rope_blocks, 0)),
        ],
        out_specs=[
            pl.BlockSpec((tm, tn), lambda i, j: (i, j)),
            pl.BlockSpec((tm, HEAD_DIM), lambda i, j: (i, 0)),
            pl.BlockSpec((2 * SUBLANES, tm), lambda i, j: (0, i)),
        ],
        out_shape=[
            jax.ShapeDtypeStruct((m, n), F32),
            jax.ShapeDtypeStruct((m, HEAD_DIM), F32),
            jax.ShapeDtypeStruct((2 * SUBLANES, m), F32),
        ],
        scratch_shapes=[pltpu.VMEM((tm, d), BF16)],
        compiler_params=_params("parallel", "arbitrary"),
        name="in_proj",
    )(x, ln, w_all, w_gates, w_ba, w_bat, cos, sin)


def _top_blocks_cols(gate, n_valid, n_top):
    n, cols = gate.shape
    blk = lax.broadcasted_iota(jnp.int32, (n, cols), 0).astype(F32)
    g = jnp.where(blk < n_valid, gate, -jnp.inf)
    picks = []
    for _ in range(n_top):
        m = jnp.max(g, axis=0, keepdims=True)
        cand = jnp.where((g == m) & (m > -jnp.inf), blk, float(n))
        idx = jnp.min(cand, axis=0, keepdims=True)
        picks.append(idx)
        g = jnp.where(blk == idx, -jnp.inf, g)
    return picks


def _moba_prompt_kernel(pj_ref, pq_ref, q_ref, k_ref, v_ref, o_ref,
                        kb_ref, vt_ref, qt_ref, kmean_ref, picks_ref, m_ref, l_ref, acc_ref,
                        *, n_blocks, n_groups, unroll):
    blk = MOBA_BLOCK
    n_top = min(MOBA_TOPK, n_blocks)
    score_scale = (HEAD_DIM ** -0.5) * LOG2_E

    def block_rows(j):
        return pl.ds(pl.multiple_of(j * blk, blk), blk)

    def prepare_block(j, carry):
        kj = k_ref[block_rows(j), :]
        kb_ref[block_rows(j), :] = kj.astype(BF16)
        vt_ref[j] = v_ref[block_rows(j), :].T.astype(BF16)
        kmean_ref[pl.ds(j, 1), :] = jnp.mean(kj, axis=0, keepdims=True)
        return carry

    lax.fori_loop(0, n_blocks, prepare_block, 0)

    key = lax.broadcasted_iota(jnp.int32, (blk, blk), 0)
    qry = lax.broadcasted_iota(jnp.int32, (blk, blk), 1)
    no_pick = jnp.full((SUBLANES - n_top, blk), -1.0, F32)

    def init_tile(qb):
        q_t = q_ref[block_rows(qb), :].T
        gate = jnp.dot(kmean_ref[...], q_t, preferred_element_type=F32, precision=lax.Precision.HIGHEST)
        picks = _top_blocks_cols(gate, lax.convert_element_type(qb, F32), n_top)
        picks_ref[qb] = jnp.concatenate(picks + [no_pick], axis=0)
        q16 = (q_t * score_scale).astype(BF16)
        qt_ref[qb] = q16
        s = jnp.dot(kb_ref[block_rows(qb), :], q16, preferred_element_type=F32)
        s = jnp.where(key <= qry, s, -jnp.inf)
        m = jnp.max(s, axis=0, keepdims=True)
        p = jnp.exp2(s - m)
        m_ref[qb] = m
        l_ref[qb] = jnp.sum(p, axis=0, keepdims=True)
        acc_ref[qb] = jnp.dot(vt_ref[qb], p.astype(BF16), preferred_element_type=F32)

    def init_tiles(i, carry):
        for u in range(unroll):
            init_tile(i * unroll + u)
        return carry

    lax.fori_loop(0, n_blocks // unroll, init_tiles, 0)

    picks_ref[n_blocks] = jnp.full((SUBLANES, blk), -1.0, F32)
    qt_ref[n_blocks] = jnp.zeros((HEAD_DIM, blk), BF16)
    m_ref[n_blocks] = jnp.zeros((1, blk), F32)
    l_ref[n_blocks] = jnp.zeros((1, blk), F32)
    acc_ref[n_blocks] = jnp.zeros((HEAD_DIM, blk), F32)

    def pair_group(g, carry):
        lanes = range(unroll)
        js = [pj_ref[g * unroll + u] for u in lanes]
        qs = [pq_ref[g * unroll + u] for u in lanes]
        m_old = [m_ref[qs[u]] for u in lanes]
        l_old = [l_ref[qs[u]] for u in lanes]
        acc_old = [acc_ref[qs[u]] for u in lanes]
        picks = [picks_ref[qs[u]] for u in lanes]
        s = [jnp.dot(kb_ref[block_rows(js[u]), :], qt_ref[qs[u]], preferred_element_type=F32) for u in lanes]
        sel = []
        for u in lanes:
            jf = lax.convert_element_type(js[u], F32)
            hit = picks[u][0:1, :] == jf
            for t in range(1, n_top):
                hit = hit | (picks[u][t:t + 1, :] == jf)
            sel.append(hit)
        m_blk = [jnp.where(sel[u], jnp.max(s[u], axis=0, keepdims=True), -jnp.inf) for u in lanes]
        m_new = [jnp.maximum(m_old[u], m_blk[u]) for u in lanes]
        alpha = [jnp.exp2(m_old[u] - m_new[u]) for u in lanes]
        shift = [jnp.where(sel[u], m_new[u], jnp.inf) for u in lanes]
        p = [jnp.exp2(s[u] - shift[u]) for u in lanes]
        l_new = [alpha[u] * l_old[u] + jnp.sum(p[u], axis=0, keepdims=True) for u in lanes]
        acc_new = [alpha[u] * acc_old[u]
                   + jnp.dot(vt_ref[js[u]], p[u].astype(BF16), preferred_element_type=F32) for u in lanes]
        for u in lanes:
            m_ref[qs[u]] = m_new[u]
            l_ref[qs[u]] = l_new[u]
            acc_ref[qs[u]] = acc_new[u]
        return carry

    lax.fori_loop(0, n_groups, pair_group, 0)

    def finish_tile(qb, carry):
        o_ref[block_rows(qb), :] = (acc_ref[qb] / l_ref[qb]).T
        return carry

    lax.fori_loop(0, n_blocks, finish_tile, 0)


def _moba_pair_table(n_blocks, unroll):
    todo = {q: list(range(q)) for q in range(1, n_blocks)}
    pj, pq = [], []
    while todo:
        tiles = sorted(todo, key=lambda q: -len(todo[q]))[:unroll]
        for q in tiles:
            pj.append(todo[q].pop())
            pq.append(q)
            if not todo[q]:
                del todo[q]
        pj += [0] * (unroll - len(tiles))
        pq += [n_blocks] * (unroll - len(tiles))
    if not pj:
        pj, pq = [0] * unroll, [n_blocks] * unroll
    return np.asarray(pj, np.int32), np.asarray(pq, np.int32)


def _moba_prompt(p3, *, q_col, k_col, v_col):
    b, t, _ = p3.shape
    n_heads = 8
    nb = t // MOBA_BLOCK
    unroll = MOBA_PAIR_UNROLL if nb % MOBA_PAIR_UNROLL == 0 else 1
    pair_j, pair_q = _moba_pair_table(nb, unroll)
    kern = functools.partial(_moba_prompt_kernel, n_blocks=nb, n_groups=len(pair_j) // unroll, unroll=unroll)

    def seq(col):
        return pl.BlockSpec((None, t, HEAD_DIM), lambda bi, h, pj, pq: (bi, 0, col + h))

    grid_spec = pltpu.PrefetchScalarGridSpec(
        num_scalar_prefetch=2,
        grid=(b, n_heads),
        in_specs=[seq(q_col), seq(k_col), seq(v_col)],
        out_specs=seq(0),
        scratch_shapes=[
            pltpu.VMEM((t, HEAD_DIM), BF16),
            pltpu.VMEM((nb, HEAD_DIM, MOBA_BLOCK), BF16),
            pltpu.VMEM((nb + 1, HEAD_DIM, MOBA_BLOCK), BF16),
            pltpu.VMEM((nb, HEAD_DIM), F32),
            pltpu.VMEM((nb + 1, SUBLANES, MOBA_BLOCK), F32),
            pltpu.VMEM((nb + 1, 1, MOBA_BLOCK), F32),
            pltpu.VMEM((nb + 1, 1, MOBA_BLOCK), F32),
            pltpu.VMEM((nb + 1, HEAD_DIM, MOBA_BLOCK), F32),
        ],
    )
    return pl.pallas_call(
        kern,
        grid_spec=grid_spec,
        out_shape=jax.ShapeDtypeStruct((b, t, n_heads * HEAD_DIM), F32),
        compiler_params=_params("parallel", "parallel"),
        name="moba_prompt",
    )(jnp.asarray(pair_j), jnp.asarray(pair_q), p3, p3, p3)


def _moba_sample_kernel(pt_ref, q_ref, kn_ref, vn_ref, *refs, n_pages):
    del pt_ref
    k_refs = refs[:n_pages]
    v_refs = refs[n_pages:2 * n_pages]
    o_ref = refs[2 * n_pages]
    scale = HEAD_DIM ** -0.5
    pages_per_block = MOBA_BLOCK // PAGE_SIZE
    n_past = n_pages // pages_per_block
    q = q_ref[...]

    gates, ms, ls, accs = [], [], [], []
    for j in range(n_past):
        pages = range(j * pages_per_block, (j + 1) * pages_per_block)
        ks = [k_refs[pg][...] for pg in pages]
        ss = [jnp.sum(k * q[None], axis=-1, keepdims=True) * scale for k in ks]
        ksum = ks[0].sum(axis=0)
        m = jnp.max(ss[0], axis=0)
        for k, s in zip(ks[1:], ss[1:]):
            ksum = ksum + k.sum(axis=0)
            m = jnp.maximum(m, jnp.max(s, axis=0))
        l = None
        acc = None
        for pg, s in zip(pages, ss):
            p = jnp.exp(s - m[None])
            pl_sum = p.sum(axis=0)
            pv = (p * v_refs[pg][...]).sum(axis=0)
            l = pl_sum if l is None else l + pl_sum
            acc = pv if acc is None else acc + pv
        gates.append(jnp.sum(q * (ksum * (1.0 / MOBA_BLOCK)), axis=-1, keepdims=True))
        ms.append(m)
        ls.append(l)
        accs.append(acc)

    n_top = min(MOBA_TOPK, n_past + 1)
    remaining = list(gates)
    selected = [jnp.zeros_like(gates[0], dtype=jnp.bool_) for _ in range(n_past)]
    for _ in range(n_top):
        best = remaining[0]
        for g in remaining[1:]:
            best = jnp.maximum(best, g)
        idx = jnp.full_like(best, float(n_past))
        for j in reversed(range(n_past)):
            idx = jnp.where(remaining[j] == best, float(j), idx)
        idx = jnp.where(best > -jnp.inf, idx, float(n_past))
        for j in range(n_past):
            hit = idx == float(j)
            selected[j] = selected[j] | hit
            remaining[j] = jnp.where(hit, -jnp.inf, remaining[j])

    s_own = jnp.sum(q * kn_ref[...], axis=-1, keepdims=True) * scale
    m_tot = s_own
    for j in range(n_past):
        m_tot = jnp.maximum(m_tot, jnp.where(selected[j], ms[j], -jnp.inf))
    e_own = jnp.exp(s_own - m_tot)
    l_tot = e_own
    o = e_own * vn_ref[...]
    for j in range(n_past):
        w = jnp.where(selected[j], jnp.exp(ms[j] - m_tot), 0.0)
        l_tot = l_tot + w * ls[j]
        o = o + w * accs[j]
    o_ref[...] = o / l_tot


def _moba_sample(ph, cache_k, cache_v, page_table, layer, *, q_row, k_row, v_row):
    db = ph.shape[0]
    n_pages = page_table.shape[1]
    n_heads = cache_k.shape[3]
    assert MOBA_BLOCK % PAGE_SIZE == 0 and n_pages % (MOBA_BLOCK // PAGE_SIZE) == 0
    assert n_heads == SUBLANES
    kern = functools.partial(_moba_sample_kernel, n_pages=n_pages)

    def row_spec(row):
        return pl.BlockSpec((None, n_heads, HEAD_DIM), lambda bi, pt: (bi, row, 0))

    def page_spec(page):
        return pl.BlockSpec((None, None, PAGE_SIZE, n_heads, HEAD_DIM),
                            lambda bi, pt: (layer, pt[bi * n_pages + page], 0, 0, 0))

    grid_spec = pltpu.PrefetchScalarGridSpec(
        num_scalar_prefetch=1,
        grid=(db,),
        in_specs=([row_spec(q_row), row_spec(k_row), row_spec(v_row)]
                  + [page_spec(pg) for pg in range(n_pages)] * 2),
        out_specs=pl.BlockSpec((None, n_heads, HEAD_DIM), lambda bi, pt: (bi, 0, 0)),
    )
    return pl.pallas_call(
        kern,
        grid_spec=grid_spec,
        out_shape=jax.ShapeDtypeStruct((db, n_heads, HEAD_DIM), F32),
        compiler_params=_params("parallel"),
        name="moba_sample",
    )(page_table.reshape(-1), ph, ph, ph, *([cache_k] * n_pages), *([cache_v] * n_pages))


def _l2norm(x):
    return x * lax.rsqrt(jnp.sum(x * x, axis=-1, keepdims=True) + 1e-6)


def _gated_rmsnorm(o, z, w):
    on = o * lax.rsqrt(jnp.mean(o * o, axis=-1, keepdims=True) + EPS) * w
    return on * _silu(z)


def _gdn_prompt_kernel(q_ref, k_ref, v_ref, qh_ref, kh_ref, vh_ref, z_ref, bat_ref,
                       cw_ref, alog_c_ref, dtb_c_ref, norm_ref,
                       y_ref, s_out_ref, s_ref, *, n_heads, n_tiles):
    c = pl.program_id(1)
    ch = GDN_CHUNK
    tile = GDN_TILE
    n_sub = tile // ch
    heads = range(n_heads)

    @pl.when(c == 0)
    def _():
        s_ref[...] = jnp.zeros_like(s_ref)

    def conv(u_ref, halo_ref, part):
        halo = jnp.where(c == 0, 0.0, halo_ref[...])
        ext = jnp.concatenate([halo, u_ref[...]], axis=0)
        w = cw_ref[:, part * n_heads * HEAD_DIM:(part + 1) * n_heads * HEAD_DIM]
        y = None
        for i in range(CONV_W):
            shift = CONV_W - 1 - i
            tap = ext if shift == 0 else pltpu.roll(ext, shift, 0)
            term = tap[SUBLANES:, :] * w[i:i + 1, :]
            y = term if y is None else y + term
        return _silu(y)

    cq = conv(q_ref, qh_ref, 0)
    ck = conv(k_ref, kh_ref, 1)
    cv = conv(v_ref, vh_ref, 2)

    ri = lax.broadcasted_iota(jnp.int32, (tile, tile), 0)
    ci = lax.broadcasted_iota(jnp.int32, (tile, tile), 1)
    chunk_shift = ch.bit_length() - 1
    same_chunk = jnp.right_shift(ri, chunk_shift) == jnp.right_shift(ci, chunk_shift)
    incl = same_chunk & (ri >= ci)
    strict = same_chunk & (ri > ci)
    eye = (ri == ci).astype(F32)
    row_chunk = jnp.right_shift(lax.broadcasted_iota(jnp.int32, (tile, 1), 0), chunk_shift)

    beta_rows = _sigmoid(bat_ref[:n_heads, :])
    g_rows = -jnp.exp(alog_c_ref[...]) * _softplus(bat_ref[n_heads:, :] + dtb_c_ref[...])
    gc_rows = _mm_exact_rhs(g_rows, (same_chunk & (ri <= ci)).astype(BF16))
    cols = _mm_exact_lhs(eye.astype(BF16), jnp.concatenate([beta_rows, gc_rows], axis=0), NT_DIMS)

    sls = [slice(h * HEAD_DIM, (h + 1) * HEAD_DIM) for h in heads]
    q = [_l2norm(cq[:, sl]) * (HEAD_DIM ** -0.5) for sl in sls]
    k = [_l2norm(ck[:, sl]) for sl in sls]
    v = [cv[:, sl] for sl in sls]
    beta = [cols[:, h:h + 1] for h in heads]
    gcc = [cols[:, n_heads + h:n_heads + h + 1] for h in heads]
    dec_incl = [jnp.exp(jnp.where(incl, gcc[h] - gc_rows[h:h + 1, :], -jnp.inf)) for h in heads]
    egc = [jnp.exp(g) for g in gcc]
    g_last = []
    g_last_col = []
    for h in heads:
        lasts = [gcc[h][(i + 1) * ch - 1:(i + 1) * ch, :] for i in range(n_sub)]
        col = lasts[n_sub - 1]
        for i in reversed(range(n_sub - 1)):
            col = jnp.where(row_chunk == i, lasts[i], col)
        g_last.append(lasts)
        g_last_col.append(col)
    k_dec = [k[h] * jnp.exp(g_last_col[h] - gcc[h]) for h in heads]

    qk_kk = [_mm(jnp.concatenate([q[h], k[h]], axis=0), k[h], NT_DIMS) for h in heads]
    qk = [qk_kk[h][:tile] * dec_incl[h] for h in heads]
    x = [-(beta[h] * qk_kk[h][tile:] * jnp.where(strict, dec_incl[h], 0.0)) for h in heads]
    t_inv = [eye + x[h] for h in heads]
    power = 2
    while power < ch:
        x = [_mm(xh, xh) for xh in x]
        t_inv = [t_inv[h] + _mm(t_inv[h], x[h]) for h in heads]
        power *= 2
    sol = [_mm(t_inv[h], jnp.concatenate([beta[h] * v[h], (beta[h] * egc[h]) * k[h]], axis=1)) for h in heads]
    u_v = [s[:, :HEAD_DIM] for s in sol]
    w = [s[:, HEAD_DIM:] for s in sol]
    q_dec = [q[h] * egc[h] for h in heads]

    state = [s_ref[h] for h in heads]
    u_parts = [[] for _ in heads]
    qs_parts = [[] for _ in heads]
    for i in range(n_sub):
        rows = slice(i * ch, (i + 1) * ch)
        wq_s = [_mm(jnp.concatenate([w[h][rows], q_dec[h][rows]], axis=0), state[h]) for h in heads]
        u = [u_v[h][rows] - wq_s[h][:ch] for h in heads]
        state = [jnp.exp(g_last[h][i]) * state[h] + _mm(k_dec[h][rows], u[h], TN_DIMS) for h in heads]
        for h in heads:
            u_parts[h].append(u[h])
            qs_parts[h].append(wq_s[h][ch:])
    for h in heads:
        s_ref[h] = state[h]
        o = jnp.concatenate(qs_parts[h], axis=0) + _mm(qk[h], jnp.concatenate(u_parts[h], axis=0))
        y_ref[:, sls[h]] = _gated_rmsnorm(o, z_ref[:, sls[h]], norm_ref[...])

    @pl.when(c == n_tiles - 1)
    def _():
        s_out_ref[...] = s_ref[...]


def _gdn_prompt(p3, bat3, conv_w, alog_c, dtb_c, norm, *, qkv_col, z_col):
    b, t, _ = p3.shape
    n_heads = 8
    width = n_heads * HEAD_DIM
    tile = GDN_TILE
    n_tiles = t // tile
    halo_per_tile = tile // SUBLANES
    kern = functools.partial(_gdn_prompt_kernel, n_heads=n_heads, n_tiles=n_tiles)

    def rows(col):
        return pl.BlockSpec((None, tile, width), lambda bi, c: (bi, c, col))

    def halo(col):
        return pl.BlockSpec((None, SUBLANES, width),
                            lambda bi, c: (bi, jnp.maximum(c * halo_per_tile - 1, 0), col))

    def const(shape):
        return pl.BlockSpec(shape, lambda bi, c: tuple(0 for _ in shape))

    return pl.pallas_call(
        kern,
        grid=(b, n_tiles),
        in_specs=[
            rows(qkv_col), rows(qkv_col + 1), rows(qkv_col + 2),
            halo(qkv_col), halo(qkv_col + 1), halo(qkv_col + 2),
            rows(z_col),
            pl.BlockSpec((None, 2 * SUBLANES, tile), lambda bi, c: (bi * n_tiles + c, 0, 0)),
            const((CONV_W, 3 * width)),
            const((n_heads, 1)), const((n_heads, 1)),
            const((1, HEAD_DIM)),
        ],
        out_specs=[
            pl.BlockSpec((None, tile, width), lambda bi, c: (bi, c, 0)),
            pl.BlockSpec((None, n_heads, HEAD_DIM, HEAD_DIM), lambda bi, c: (bi, 0, 0, 0)),
        ],
        out_shape=[
            jax.ShapeDtypeStruct((b, t, width), F32),
            jax.ShapeDtypeStruct((b, n_heads, HEAD_DIM, HEAD_DIM), F32),
        ],
        scratch_shapes=[pltpu.VMEM((n_heads, HEAD_DIM, HEAD_DIM), F32)],
        compiler_params=_params("parallel", "arbitrary"),
        name="gdn_prompt",
    )(p3, p3, p3, p3, p3, p3, p3, bat3, conv_w, alog_c, dtb_c, norm)


def _gdn_sample_kernel(qkv_ref, conv_ref, z_ref, ba_ref, cw_ref, alog_l_ref, dtb_l_ref, norm_ref, s_ref,
                       y_ref, s_out_ref, *, n_heads):
    width3 = qkv_ref.shape[1]
    width = width3 // 3
    rows = qkv_ref.shape[0]
    w = cw_ref[...]
    y = conv_ref[:, 0:width3] * w[0:1, :]
    for i in range(1, CONV_W - 1):
        y = y + conv_ref[:, i * width3:(i + 1) * width3] * w[i:i + 1, :]
    y = y + qkv_ref[...] * w[CONV_W - 1:CONV_W, :]
    c = _silu(y)

    ba = ba_ref[...]
    beta_cols = _sigmoid(ba)
    g_cols = -jnp.exp(alog_l_ref[...]) * _softplus(ba + dtb_l_ref[...])
    decay_cols = jnp.exp(g_cols)
    eye = (lax.broadcasted_iota(jnp.int32, (HEAD_DIM, HEAD_DIM), 0)
           == lax.broadcasted_iota(jnp.int32, (HEAD_DIM, HEAD_DIM), 1)).astype(BF16)

    for h in range(n_heads):
        q = _l2norm(c[:, h * HEAD_DIM:(h + 1) * HEAD_DIM]) * (HEAD_DIM ** -0.5)
        k = _l2norm(c[:, width + h * HEAD_DIM:width + (h + 1) * HEAD_DIM])
        v = c[:, 2 * width + h * HEAD_DIM:2 * width + (h + 1) * HEAD_DIM]
        q_t = _mm_exact_lhs(eye, q, NT_DIMS)
        k_t = _mm_exact_lhs(eye, k, NT_DIMS)
        seqs = range(rows)
        k_col = [k_t[:, r:r + 1] for r in seqs]
        s = [decay_cols[r:r + 1, n_heads + h:n_heads + h + 1] * s_ref[r, h] for r in seqs]
        ks = [jnp.sum(k_col[r] * s[r], axis=0, keepdims=True) for r in seqs]
        u = [beta_cols[r:r + 1, h:h + 1] * (v[r:r + 1, :] - ks[r]) for r in seqs]
        s = [s[r] + k_col[r] * u[r] for r in seqs]
        for r in seqs:
            s_out_ref[r, h] = s[r]
        o = jnp.concatenate([jnp.sum(q_t[:, r:r + 1] * s[r], axis=0, keepdims=True) for r in seqs], axis=0)
        sl = slice(h * HEAD_DIM, (h + 1) * HEAD_DIM)
        y_ref[:, sl] = _gated_rmsnorm(o, z_ref[:, sl], norm_ref[...])


def _gdn_sample(p2, conv_prev2, ba2, states, layer, conv_w, alog_l, dtb_l, norm, *, qkv_col, z_col):
    db = p2.shape[0]
    n_heads = states.shape[2]
    width = n_heads * HEAD_DIM
    rows = SUBLANES
    kern = functools.partial(_gdn_sample_kernel, n_heads=n_heads)

    def const(shape):
        return pl.BlockSpec(shape, lambda i: tuple(0 for _ in shape))

    return pl.pallas_call(
        kern,
        grid=(db // rows,),
        in_specs=[
            pl.BlockSpec((rows, 3 * width), lambda i: (i, qkv_col // 3)),
            pl.BlockSpec((rows, (CONV_W - 1) * 3 * width), lambda i: (i, 0)),
            pl.BlockSpec((rows, width), lambda i: (i, z_col)),
            pl.BlockSpec((rows, HEAD_DIM), lambda i: (i, 0)),
            const((CONV_W, 3 * width)),
            const((1, HEAD_DIM)), const((1, HEAD_DIM)), const((1, HEAD_DIM)),
            pl.BlockSpec((None, rows, n_heads, HEAD_DIM, HEAD_DIM), lambda i: (layer, i, 0, 0, 0)),
        ],
        out_specs=[
            pl.BlockSpec((rows, width), lambda i: (i, 0)),
            pl.BlockSpec((rows, n_heads, HEAD_DIM, HEAD_DIM), lambda i: (i, 0, 0, 0)),
        ],
        out_shape=[
            jax.ShapeDtypeStruct((db, width), F32),
            jax.ShapeDtypeStruct(states.shape[1:], F32),
        ],
        compiler_params=_params("parallel"),
        name="gdn_sample",
    )(p2, conv_prev2, p2, ba2, conv_w, alog_l, dtb_l, norm, states)


def _merge_kernel(x_ref, yg_ref, ym_ref, sg_ref, sm_ref, wpg_ref, wpm_ref, wo_ref, o_ref):
    yg = jnp.dot(yg_ref[...].astype(BF16), wpg_ref[...], preferred_element_type=F32)
    ym = jnp.dot(ym_ref[...].astype(BF16), wpm_ref[...], preferred_element_type=F32)
    mixed = sg_ref[...] * yg + sm_ref[...] * ym
    o_ref[...] = x_ref[...] + jnp.dot(mixed.astype(BF16), wo_ref[...], preferred_element_type=F32)


def _merge(x, yg, ym, p, w_pg, w_pm, w_o, layer, *, tm, sg_col, sm_col):
    m, d = x.shape

    def rows(col):
        return pl.BlockSpec((tm, d), lambda i: (i, col))

    def weight():
        return pl.BlockSpec((None, d, d), lambda i: (layer, 0, 0))

    return pl.pallas_call(
        _merge_kernel,
        grid=(m // tm,),
        in_specs=[rows(0), rows(0), rows(0), rows(sg_col), rows(sm_col), weight(), weight(), weight()],
        out_specs=rows(0),
        out_shape=jax.ShapeDtypeStruct((m, d), F32),
        compiler_params=_params("parallel"),
        name="merge",
    )(x, yg, ym, p, p, w_pg, w_pm, w_o)


def _ffn_kernel(x_ref, ln_ref, wg_ref, wu_ref, wd_ref, lnf_ref, o_ref, xn_ref, acc_ref, *, final):
    j = pl.program_id(1)

    @pl.when(j == 0)
    def _():
        x = x_ref[...]
        y = x * lax.rsqrt(jnp.mean(x * x, axis=-1, keepdims=True) + EPS)
        xn_ref[...] = (y * ln_ref[...]).astype(BF16)
        acc_ref[...] = jnp.zeros_like(acc_ref)

    xn = xn_ref[...]
    g = jnp.dot(xn, wg_ref[...], preferred_element_type=F32)
    u = jnp.dot(xn, wu_ref[...], preferred_element_type=F32)
    acc_ref[...] += jnp.dot((_silu(g) * u).astype(BF16), wd_ref[...], preferred_element_type=F32)

    @pl.when(j == pl.num_programs(1) - 1)
    def _():
        out = x_ref[...] + acc_ref[...]
        if final:
            out = out * lax.rsqrt(jnp.mean(out * out, axis=-1, keepdims=True) + EPS) * lnf_ref[...]
        o_ref[...] = out


def _ffn(x, ln, w_gu, w_dn, ln_final, layer, *, tm, tf, final):
    m, d = x.shape
    d_ff = w_dn.shape[1]
    n_f = d_ff // tf
    return pl.pallas_call(
        functools.partial(_ffn_kernel, final=final),
        grid=(m // tm, n_f),
        in_specs=[
            pl.BlockSpec((tm, d), lambda i, j: (i, 0)),
            pl.BlockSpec((1, d), lambda i, j: (0, 0)),
            pl.BlockSpec((None, d, tf), lambda i, j: (layer, 0, j)),
            pl.BlockSpec((None, d, tf), lambda i, j: (layer, 0, n_f + j)),
            pl.BlockSpec((None, tf, d), lambda i, j: (layer, j, 0)),
            pl.BlockSpec((1, d), lambda i, j: (0, 0)),
        ],
        out_specs=pl.BlockSpec((tm, d), lambda i, j: (i, 0)),
        out_shape=jax.ShapeDtypeStruct((m, d), F32),
        scratch_shapes=[pltpu.VMEM((tm, d), BF16), pltpu.VMEM((tm, d), F32)],
        compiler_params=_params("parallel", "arbitrary"),
        name="ffn",
    )(x, ln, w_gu, w_gu, w_dn, ln_final)


def _rope_tables(pos):
    half = HEAD_DIM // 2
    inv = jnp.exp(-math.log(ROPE_THETA) * jnp.arange(half, dtype=F32) / half)
    ang = pos.astype(F32)[:, None] * inv[None, :]
    cos = jnp.cos(ang)
    sin = jnp.sin(ang)
    return jnp.concatenate([cos, cos], axis=-1), jnp.concatenate([-sin, sin], axis=-1)


def _lane_row(vec, offset):
    return jnp.zeros((1, HEAD_DIM), F32).at[0, offset:offset + vec.shape[0]].set(vec.astype(F32))


def kernel(x_prompt, x_sample, cache_k, cache_v, state_ssm, state_conv, page_table, ln_mix, ln_ffn, w_in,
           conv_w, a_log, dt_bias, gdn_norm, w_branch_gdn, w_branch_moba, w_out, w_gate_up, w_down, ln_final):
    b, t, d = x_prompt.shape
    db, ts, _ = x_sample.shape
    depth = w_in.shape[0]
    n_heads = d // HEAD_DIM
    width = n_heads * HEAD_DIM
    n_pages = page_table.shape[1]
    past = n_pages * PAGE_SIZE
    d_ff = w_down.shape[1]
    assert ts == 1 and n_heads == 8 and width == d
    assert t % MOBA_BLOCK == 0 and t % 1024 == 0 and db % SUBLANES == 0

    c_ba = 7 * width
    c_g = c_ba + 2 * n_heads
    col_q, col_k, col_v, col_qkv, col_z, col_sg, col_sm = 0, 1, 2, 3, 6, 7, 8

    cos_p, sin_p = _rope_tables(jnp.arange(t, dtype=jnp.int32))
    cos_s, sin_s = _rope_tables(jnp.full((db,), past, dtype=jnp.int32))

    tm_p = 1024
    tf = d_ff // 2 if (d_ff // 2) % HEAD_DIM == 0 else d_ff
    xp = x_prompt.reshape(b * t, d)
    xs = x_sample.reshape(db * ts, d)
    outs = {name: [] for name in ("kp", "vp", "sp", "cp", "ks", "vs", "ss", "cs")}
    w_all = w_in.astype(BF16)
    w_gates = w_in[:, :, c_g:].astype(BF16)
    w_ba = jnp.pad(w_in[:, :, c_ba:c_g], ((0, 0), (0, 0), (0, HEAD_DIM - 2 * n_heads))).astype(BF16)
    w_bat = jnp.swapaxes(w_in[:, :, c_ba:c_g], 1, 2).astype(BF16)
    w_pg = w_branch_gdn.astype(BF16)
    w_pm = w_branch_moba.astype(BF16)
    w_o = w_out.astype(BF16)
    w_gu = w_gate_up.astype(BF16)
    w_dn = w_down.astype(BF16)
    for l in range(depth):
        last = l == depth - 1
        ln_m = ln_mix[l].reshape(1, d)
        ln_f = ln_ffn[l].reshape(1, d)
        alog_l = _lane_row(a_log[l], n_heads)
        dtb_l = _lane_row(dt_bias[l], n_heads)
        alog_c = a_log[l].reshape(n_heads, 1).astype(F32)
        dtb_c = dt_bias[l].reshape(n_heads, 1).astype(F32)
        norm = gdn_norm[l].reshape(1, HEAD_DIM)
        proj = functools.partial(_in_proj, tn=width, n_rope_tiles=2, n_lead_tiles=col_sg)

        p, ba, bat = proj(xp, ln_m, w_all, w_gates, w_ba, w_bat, cos_p, sin_p, l, tm=tm_p)
        p3 = p.reshape(b, t, -1)
        ym = _moba_prompt(p3, q_col=col_q * n_heads, k_col=col_k * n_heads, v_col=col_v * n_heads)
        bat3 = bat.reshape(2 * n_heads, b * t // GDN_TILE, GDN_TILE).transpose(1, 0, 2)
        yg, s_fin = _gdn_prompt(p3, bat3, conv_w[l], alog_c, dtb_c, norm, qkv_col=col_qkv, z_col=col_z)
        x1 = _merge(xp, yg.reshape(b * t, width), ym.reshape(b * t, width), p, w_pg, w_pm, w_o, l,
                    tm=tm_p // 2, sg_col=col_sg, sm_col=col_sm)
        xp = _ffn(x1, ln_f, w_gu, w_dn, ln_final.reshape(1, d), l, tm=tm_p, tf=tf, final=last)
        outs["kp"].append(p3[:, :, col_k * width:(col_k + 1) * width])
        outs["vp"].append(p3[:, :, col_v * width:(col_v + 1) * width])
        outs["sp"].append(s_fin)
        outs["cp"].append(p3[:, t - (CONV_W - 1):, col_qkv * width:(col_qkv + 3) * width])

        p, ba, bat = proj(xs, ln_m, w_all, w_gates, w_ba, w_bat, cos_s, sin_s, l, tm=db)
        ph = p.reshape(db, -1, HEAD_DIM)
        ym = _moba_sample(ph, cache_k, cache_v, page_table, l, q_row=col_q, k_row=col_k, v_row=col_v)
        yg, s_new = _gdn_sample(p, state_conv[l].reshape(db, -1), ba, state_ssm, l, conv_w[l], alog_l, dtb_l,
                                norm, qkv_col=col_qkv, z_col=col_z)
        x1 = _merge(xs, yg, ym.reshape(db, width), p, w_pg, w_pm, w_o, l, tm=db, sg_col=col_sg, sm_col=col_sm)
        xs = _ffn(x1, ln_f, w_gu, w_dn, ln_final.reshape(1, d), l, tm=db, tf=tf, final=last)
        qkv_new = p[:, col_qkv * width:(col_qkv + 3) * width]
        outs["ks"].append(p[:, col_k * width:(col_k + 1) * width])
        outs["vs"].append(p[:, col_v * width:(col_v + 1) * width])
        outs["ss"].append(s_new)
        outs["cs"].append(jnp.concatenate([state_conv[l][:, 1:], qkv_new[:, None, :]], axis=1))

    n_tpages = t // PAGE_SIZE
    return (
        xp.reshape(b, t, d),
        xs.reshape(db, ts, d),
        jnp.stack(outs["kp"]).reshape(depth, b, n_tpages, PAGE_SIZE, n_heads, HEAD_DIM),
        jnp.stack(outs["vp"]).reshape(depth, b, n_tpages, PAGE_SIZE, n_heads, HEAD_DIM),
        jnp.stack(outs["sp"]),
        jnp.stack(outs["cp"]),
        jnp.stack(outs["ks"]).reshape(depth, db, ts, n_heads, HEAD_DIM),
        jnp.stack(outs["vs"]).reshape(depth, db, ts, n_heads, HEAD_DIM),
        jnp.stack(outs["ss"]),
        jnp.stack(outs["cs"]),
    )
```

```python
import functools
import math

import jax
import jax.numpy as jnp
import numpy as np
from jax import lax
from jax.experimental import pallas as pl
from jax.experimental.pallas import tpu as pltpu

F32 = jnp.float32
BF16 = jnp.bfloat16

HEAD_DIM = 128
MOBA_BLOCK = 256
MOBA_TOPK = 3
MOBA_PAIR_UNROLL = 8
GDN_CHUNK = 64
GDN_TILE = 2 * GDN_CHUNK
CONV_W = 4
PAGE_SIZE = 128
ROPE_THETA = 10000.0
EPS = 1e-6
SUBLANES = 8
VMEM_LIMIT_BYTES = 50 * 1024 * 1024
LOG2_E = 1.4426950408889634

NT_DIMS = (((1,), (1,)), ((), ()))
TN_DIMS = (((0,), (0,)), ((), ()))


def _params(*semantics):
    return pltpu.CompilerParams(dimension_semantics=semantics, vmem_limit_bytes=VMEM_LIMIT_BYTES)


def _mm(a, b, dims=None):
    a = a.astype(BF16)
    b = b.astype(BF16)
    if dims is None:
        return jnp.dot(a, b, preferred_element_type=F32)
    return lax.dot_general(a, b, dims, preferred_element_type=F32)


def _split3(x):
    hi = x.astype(BF16)
    r1 = x - hi.astype(F32)
    mid = r1.astype(BF16)
    lo = (r1 - mid.astype(F32)).astype(BF16)
    return hi, mid, lo


def _mm_exact_rhs(a, b_exact, dims=None):
    out = None
    for part in _split3(a):
        t = _mm(part, b_exact, dims)
        out = t if out is None else out + t
    return out


def _mm_exact_lhs(a_exact, b, dims=None):
    out = None
    for part in _split3(b):
        t = _mm(a_exact, part, dims)
        out = t if out is None else out + t
    return out


def _sigmoid(x):
    return 1.0 / (1.0 + jnp.exp(-x))


def _silu(x):
    return x * _sigmoid(x)


def _softplus(x):
    return jnp.maximum(x, 0.0) + jnp.log1p(jnp.exp(-jnp.abs(x)))


def _in_proj_kernel(x_ref, ln_ref, wa_ref, wg_ref, wba_ref, wbat_ref, cos_ref, sin_ref,
                    p_ref, ba_ref, bat_ref, k_ref, v_ref, xn_ref, *, n_rope_tiles, n_lead_tiles, k_tile, v_tile):
    j = pl.program_id(1)

    @pl.when(j == 0)
    def _():
        x = x_ref[...]
        y = x * lax.rsqrt(jnp.mean(x * x, axis=-1, keepdims=True) + EPS)
        xn = (y * ln_ref[...]).astype(BF16)
        xn_ref[...] = xn
        ba_ref[...] = jnp.dot(xn, wba_ref[...], preferred_element_type=F32)
        bat_ref[...] = lax.dot_general(wbat_ref[...], xn, NT_DIMS, preferred_element_type=F32)

    @pl.when(j < n_rope_tiles)
    def _():
        acc = jnp.dot(xn_ref[...], wa_ref[...], preferred_element_type=F32)
        cos = cos_ref[...]
        sin = sin_ref[...]
        for h in range(acc.shape[1] // HEAD_DIM):
            sl = slice(h * HEAD_DIM, (h + 1) * HEAD_DIM)
            xh = acc[:, sl]
            p_ref[:, sl] = xh * cos + pltpu.roll(xh, HEAD_DIM // 2, 1) * sin

        @pl.when(j == k_tile)
        def _():
            k_ref[...] = p_ref[...]

    @pl.when((j >= n_rope_tiles) & (j < n_lead_tiles))
    def _():
        p_ref[...] = jnp.dot(xn_ref[...], wa_ref[...], preferred_element_type=F32)

        @pl.when(j == v_tile)
        def _():
            v_ref[...] = p_ref[...]

    @pl.when(j >= n_lead_tiles)
    def _():
        p_ref[...] = _sigmoid(jnp.dot(xn_ref[...], wg_ref[...], preferred_element_type=F32))


def _in_proj(x, ln, w_all, w_gates, w_ba, w_bat, cos, sin, layer, *, tm, tn, n_rope_tiles, n_lead_tiles,
             k_tile, v_tile):
    m, d = x.shape
    n_tiles = n_lead_tiles + w_gates.shape[2] // tn
    rope_blocks = cos.shape[0] // tm
    assert k_tile < n_rope_tiles <= v_tile < n_lead_tiles
    kern = functools.partial(_in_proj_kernel, n_rope_tiles=n_rope_tiles, n_lead_tiles=n_lead_tiles,
                             k_tile=k_tile, v_tile=v_tile)
    n = n_tiles * tn
    return pl.pallas_call(
        kern,
        grid=(m // tm, n_tiles),
        in_specs=[
            pl.BlockSpec((tm, d), lambda i, j: (i, 0)),
            pl.BlockSpec((1, d), lambda i, j: (0, 0)),
            pl.BlockSpec((None, d, tn), lambda i, j: (layer, 0, jnp.minimum(j, n_lead_tiles - 1))),
            pl.BlockSpec((None, d, tn), lambda i, j: (layer, 0, jnp.maximum(j - n_lead_tiles, 0))),
            pl.BlockSpec((None, d, HEAD_DIM), lambda i, j: (layer, 0, 0)),
            pl.BlockSpec((None, 2 * SUBLANES, d), lambda i, j: (layer, 0, 0)),
            pl.BlockSpec((tm, HEAD_DIM), lambda i, j: (i % rope_blocks, 0)),
            pl.BlockSpec((tm, HEAD_DIM), lambda i, j: (i % rope_blocks, 0)),
        ],
        out_specs=[
            pl.BlockSpec((tm, tn), lambda i, j: (i, j)),
            pl.BlockSpec((tm, HEAD_DIM), lambda i, j: (i, 0)),
            pl.BlockSpec((2 * SUBLANES, tm), lambda i, j: (0, i)),
            pl.BlockSpec((tm, tn), lambda i, j: (i, 0)),
            pl.BlockSpec((tm, tn), lambda i, j: (i, 0)),
        ],
        out_shape=[
            jax.ShapeDtypeStruct((m, n), F32),
            jax.ShapeDtypeStruct((m, HEAD_DIM), F32),
            jax.ShapeDtypeStruct((2 * SUBLANES, m), F32),
            jax.ShapeDtypeStruct((m, tn), F32),
            jax.ShapeDtypeStruct((m, tn), F32),
        ],
        scratch_shapes=[pltpu.VMEM((tm, d), BF16)],
        compiler_params=_params("parallel", "arbitrary"),
        name="in_proj",
    )(x, ln, w_all, w_gates, w_ba, w_bat, cos, sin)


def _top_blocks_cols(gate, n_valid, n_top):
    n, cols = gate.shape
    blk = lax.broadcasted_iota(jnp.int32, (n, cols), 0).astype(F32)
    g = jnp.where(blk < n_valid, gate, -jnp.inf)
    picks = []
    for _ in range(n_top):
        m = jnp.max(g, axis=0, keepdims=True)
        cand = jnp.where((g == m) & (m > -jnp.inf), blk, float(n))
        idx = jnp.min(cand, axis=0, keepdims=True)
        picks.append(idx)
        g = jnp.where(blk == idx, -jnp.inf, g)
    return picks


def _moba_prompt_kernel(pj_ref, pq_ref, q_ref, k_ref, v_ref, o_ref,
                        kb_ref, vt_ref, qt_ref, kmean_ref, picks_ref, m_ref, l_ref, acc_ref,
                        *, n_blocks, n_groups, unroll):
    blk = MOBA_BLOCK
    n_top = min(MOBA_TOPK, n_blocks)
    score_scale = (HEAD_DIM ** -0.5) * LOG2_E

    def block_rows(j):
        return pl.ds(pl.multiple_of(j * blk, blk), blk)

    def prepare_block(j, carry):
        kj = k_ref[block_rows(j), :]
        kb_ref[block_rows(j), :] = kj.astype(BF16)
        vt_ref[j] = v_ref[block_rows(j), :].T.astype(BF16)
        kmean_ref[pl.ds(j, 1), :] = jnp.mean(kj, axis=0, keepdims=True)
        return carry

    lax.fori_loop(0, n_blocks, prepare_block, 0)

    key = lax.broadcasted_iota(jnp.int32, (blk, blk), 0)
    qry = lax.broadcasted_iota(jnp.int32, (blk, blk), 1)
    no_pick = jnp.full((SUBLANES - n_top, blk), -1.0, F32)

    def init_tile(qb):
        q_t = q_ref[block_rows(qb), :].T
        gate = jnp.dot(kmean_ref[...], q_t, preferred_element_type=F32, precision=lax.Precision.HIGHEST)
        picks = _top_blocks_cols(gate, lax.convert_element_type(qb, F32), n_top)
        picks_ref[qb] = jnp.concatenate(picks + [no_pick], axis=0)
        q16 = (q_t * score_scale).astype(BF16)
        qt_ref[qb] = q16
        s = jnp.dot(kb_ref[block_rows(qb), :], q16, preferred_element_type=F32)
        s = jnp.where(key <= qry, s, -jnp.inf)
        m = jnp.max(s, axis=0, keepdims=True)
        p = jnp.exp2(s - m)
        m_ref[qb] = m
        l_ref[qb] = jnp.sum(p, axis=0, keepdims=True)
        acc_ref[qb] = jnp.dot(vt_ref[qb], p.astype(BF16), preferred_element_type=F32)

    def init_tiles(i, carry):
        for u in range(unroll):
            init_tile(i * unroll + u)
        return carry

    lax.fori_loop(0, n_blocks // unroll, init_tiles, 0)

    picks_ref[n_blocks] = jnp.full((SUBLANES, blk), -1.0, F32)
    qt_ref[n_blocks] = jnp.zeros((HEAD_DIM, blk), BF16)
    m_ref[n_blocks] = jnp.zeros((1, blk), F32)
    l_ref[n_blocks] = jnp.zeros((1, blk), F32)
    acc_ref[n_blocks] = jnp.zeros((HEAD_DIM, blk), F32)

    def pair_group(g, carry):
        lanes = range(unroll)
        js = [pj_ref[g * unroll + u] for u in lanes]
        qs = [pq_ref[g * unroll + u] for u in lanes]
        m_old = [m_ref[qs[u]] for u in lanes]
        l_old = [l_ref[qs[u]] for u in lanes]
        acc_old = [acc_ref[qs[u]] for u in lanes]
        picks = [picks_ref[qs[u]] for u in lanes]
        s = [jnp.dot(kb_ref[block_rows(js[u]), :], qt_ref[qs[u]], preferred_element_type=F32) for u in lanes]
        sel = []
        for u in lanes:
            jf = lax.convert_element_type(js[u], F32)
            hit = picks[u][0:1, :] == jf
            for t in range(1, n_top):
                hit = hit | (picks[u][t:t + 1, :] == jf)
            sel.append(hit)
        m_blk = [jnp.where(sel[u], jnp.max(s[u], axis=0, keepdims=True), -jnp.inf) for u in lanes]
        m_new = [jnp.maximum(m_old[u], m_blk[u]) for u in lanes]
        alpha = [jnp.exp2(m_old[u] - m_new[u]) for u in lanes]
        shift = [jnp.where(sel[u], m_new[u], jnp.inf) for u in lanes]
        p = [jnp.exp2(s[u] - shift[u]) for u in lanes]
        l_new = [alpha[u] * l_old[u] + jnp.sum(p[u], axis=0, keepdims=True) for u in lanes]
        acc_new = [alpha[u] * acc_old[u]
                   + jnp.dot(vt_ref[js[u]], p[u].astype(BF16), preferred_element_type=F32) for u in lanes]
        for u in lanes:
            m_ref[qs[u]] = m_new[u]
            l_ref[qs[u]] = l_new[u]
            acc_ref[qs[u]] = acc_new[u]
        return carry

    lax.fori_loop(0, n_groups, pair_group, 0)

    def finish_tile(qb, carry):
        o_ref[block_rows(qb), :] = (acc_ref[qb] / l_ref[qb]).T
        return carry

    lax.fori_loop(0, n_blocks, finish_tile, 0)


def _moba_pair_table(n_blocks, unroll):
    todo = {q: list(range(q)) for q in range(1, n_blocks)}
    pj, pq = [], []
    while todo:
        tiles = sorted(todo, key=lambda q: -len(todo[q]))[:unroll]
        for q in tiles:
            pj.append(todo[q].pop())
            pq.append(q)
            if not todo[q]:
                del todo[q]
        pj += [0] * (unroll - len(tiles))
        pq += [n_blocks] * (unroll - len(tiles))
    if not pj:
        pj, pq = [0] * unroll, [n_blocks] * unroll
    return np.asarray(pj, np.int32), np.asarray(pq, np.int32)


def _moba_prompt(p3, *, q_col, k_col, v_col):
    b, t, _ = p3.shape
    n_heads = 8
    nb = t // MOBA_BLOCK
    unroll = MOBA_PAIR_UNROLL if nb % MOBA_PAIR_UNROLL == 0 else 1
    pair_j, pair_q = _moba_pair_table(nb, unroll)
    kern = functools.partial(_moba_prompt_kernel, n_blocks=nb, n_groups=len(pair_j) // unroll, unroll=unroll)

    def seq(col):
        return pl.BlockSpec((None, t, HEAD_DIM), lambda bi, h, pj, pq: (bi, 0, col + h))

    grid_spec = pltpu.PrefetchScalarGridSpec(
        num_scalar_prefetch=2,
        grid=(b, n_heads),
        in_specs=[seq(q_col), seq(k_col), seq(v_col)],
        out_specs=seq(0),
        scratch_shapes=[
            pltpu.VMEM((t, HEAD_DIM), BF16),
            pltpu.VMEM((nb, HEAD_DIM, MOBA_BLOCK), BF16),
            pltpu.VMEM((nb + 1, HEAD_DIM, MOBA_BLOCK), BF16),
            pltpu.VMEM((nb, HEAD_DIM), F32),
            pltpu.VMEM((nb + 1, SUBLANES, MOBA_BLOCK), F32),
            pltpu.VMEM((nb + 1, 1, MOBA_BLOCK), F32),
            pltpu.VMEM((nb + 1, 1, MOBA_BLOCK), F32),
            pltpu.VMEM((nb + 1, HEAD_DIM, MOBA_BLOCK), F32),
        ],
    )
    return pl.pallas_call(
        kern,
        grid_spec=grid_spec,
        out_shape=jax.ShapeDtypeStruct((b, t, n_heads * HEAD_DIM), F32),
        compiler_params=_params("parallel", "parallel"),
        name="moba_prompt",
    )(jnp.asarray(pair_j), jnp.asarray(pair_q), p3, p3, p3)


def _moba_sample_kernel(pt_ref, q_ref, kn_ref, vn_ref, *refs, n_pages):
    del pt_ref
    k_refs = refs[:n_pages]
    v_refs = refs[n_pages:2 * n_pages]
    o_ref = refs[2 * n_pages]
    scale = HEAD_DIM ** -0.5
    pages_per_block = MOBA_BLOCK // PAGE_SIZE
    n_past = n_pages // pages_per_block
    q = q_ref[...]

    gates, ms, ls, accs = [], [], [], []
    for j in range(n_past):
        pages = range(j * pages_per_block, (j + 1) * pages_per_block)
        ks = [k_refs[pg][...] for pg in pages]
        ss = [jnp.sum(k * q[None], axis=-1, keepdims=True) * scale for k in ks]
        ksum = ks[0].sum(axis=0)
        m = jnp.max(ss[0], axis=0)
        for k, s in zip(ks[1:], ss[1:]):
            ksum = ksum + k.sum(axis=0)
            m = jnp.maximum(m, jnp.max(s, axis=0))
        l = None
        acc = None
        for pg, s in zip(pages, ss):
            p = jnp.exp(s - m[None])
            pl_sum = p.sum(axis=0)
            pv = (p * v_refs[pg][...]).sum(axis=0)
            l = pl_sum if l is None else l + pl_sum
            acc = pv if acc is None else acc + pv
        gates.append(jnp.sum(q * (ksum * (1.0 / MOBA_BLOCK)), axis=-1, keepdims=True))
        ms.append(m)
        ls.append(l)
        accs.append(acc)

    n_top = min(MOBA_TOPK, n_past + 1)
    remaining = list(gates)
    selected = [jnp.zeros_like(gates[0], dtype=jnp.bool_) for _ in range(n_past)]
    for _ in range(n_top):
        best = remaining[0]
        for g in remaining[1:]:
            best = jnp.maximum(best, g)
        idx = jnp.full_like(best, float(n_past))
        for j in reversed(range(n_past)):
            idx = jnp.where(remaining[j] == best, float(j), idx)
        idx = jnp.where(best > -jnp.inf, idx, float(n_past))
        for j in range(n_past):
            hit = idx == float(j)
            selected[j] = selected[j] | hit
            remaining[j] = jnp.where(hit, -jnp.inf, remaining[j])

    s_own = jnp.sum(q * kn_ref[...], axis=-1, keepdims=True) * scale
    m_tot = s_own
    for j in range(n_past):
        m_tot = jnp.maximum(m_tot, jnp.where(selected[j], ms[j], -jnp.inf))
    e_own = jnp.exp(s_own - m_tot)
    l_tot = e_own
    o = e_own * vn_ref[...]
    for j in range(n_past):
        w = jnp.where(selected[j], jnp.exp(ms[j] - m_tot), 0.0)
        l_tot = l_tot + w * ls[j]
        o = o + w * accs[j]
    o_ref[...] = o / l_tot


def _moba_sample(ph, cache_k, cache_v, page_table, layer, *, q_row, k_row, v_row):
    db = ph.shape[0]
    n_pages = page_table.shape[1]
    n_heads = cache_k.shape[3]
    assert MOBA_BLOCK % PAGE_SIZE == 0 and n_pages % (MOBA_BLOCK // PAGE_SIZE) == 0
    assert n_heads == SUBLANES
    kern = functools.partial(_moba_sample_kernel, n_pages=n_pages)

    def row_spec(row):
        return pl.BlockSpec((None, n_heads, HEAD_DIM), lambda bi, pt: (bi, row, 0))

    def page_spec(page):
        return pl.BlockSpec((None, None, PAGE_SIZE, n_heads, HEAD_DIM),
                            lambda bi, pt: (layer, pt[bi * n_pages + page], 0, 0, 0))

    grid_spec = pltpu.PrefetchScalarGridSpec(
        num_scalar_prefetch=1,
        grid=(db,),
        in_specs=([row_spec(q_row), row_spec(k_row), row_spec(v_row)]
                  + [page_spec(pg) for pg in range(n_pages)] * 2),
        out_specs=pl.BlockSpec((None, n_heads, HEAD_DIM), lambda bi, pt: (bi, 0, 0)),
    )
    return pl.pallas_call(
        kern,
        grid_spec=grid_spec,
        out_shape=jax.ShapeDtypeStruct((db, n_heads, HEAD_DIM), F32),
        compiler_params=_params("parallel"),
        name="moba_sample",
    )(page_table.reshape(-1), ph, ph, ph, *([cache_k] * n_pages), *([cache_v] * n_pages))


def _l2norm(x):
    return x * lax.rsqrt(jnp.sum(x * x, axis=-1, keepdims=True) + 1e-6)


def _gated_rmsnorm(o, z, w):
    on = o * lax.rsqrt(jnp.mean(o * o, axis=-1, keepdims=True) + EPS) * w
    return on * _silu(z)


def _gdn_prompt_kernel(q_ref, k_ref, v_ref, qh_ref, kh_ref, vh_ref, z_ref, bat_ref,
                       cw_ref, alog_c_ref, dtb_c_ref, norm_ref,
                       y_ref, s_out_ref, s_ref, *, n_heads, n_tiles):
    c = pl.program_id(1)
    ch = GDN_CHUNK
    tile = GDN_TILE
    n_sub = tile // ch
    heads = range(n_heads)

    @pl.when(c == 0)
    def _():
        s_ref[...] = jnp.zeros_like(s_ref)

    def conv(u_ref, halo_ref, part):
        halo = jnp.where(c == 0, 0.0, halo_ref[...])
        ext = jnp.concatenate([halo, u_ref[...]], axis=0)
        w = cw_ref[:, part * n_heads * HEAD_DIM:(part + 1) * n_heads * HEAD_DIM]
        y = None
        for i in range(CONV_W):
            shift = CONV_W - 1 - i
            tap = ext if shift == 0 else pltpu.roll(ext, shift, 0)
            term = tap[SUBLANES:, :] * w[i:i + 1, :]
            y = term if y is None else y + term
        return _silu(y)

    cq = conv(q_ref, qh_ref, 0)
    ck = conv(k_ref, kh_ref, 1)
    cv = conv(v_ref, vh_ref, 2)

    ri = lax.broadcasted_iota(jnp.int32, (tile, tile), 0)
    ci = lax.broadcasted_iota(jnp.int32, (tile, tile), 1)
    chunk_shift = ch.bit_length() - 1
    same_chunk = jnp.right_shift(ri, chunk_shift) == jnp.right_shift(ci, chunk_shift)
    incl = same_chunk & (ri >= ci)
    strict = same_chunk & (ri > ci)
    eye = (ri == ci).astype(F32)
    row_chunk = jnp.right_shift(lax.broadcasted_iota(jnp.int32, (tile, 1), 0), chunk_shift)

    beta_rows = _sigmoid(bat_ref[:n_heads, :])
    g_rows = -jnp.exp(alog_c_ref[...]) * _softplus(bat_ref[n_heads:, :] + dtb_c_ref[...])
    gc_rows = _mm_exact_rhs(g_rows, (same_chunk & (ri <= ci)).astype(BF16))
    cols = _mm_exact_lhs(eye.astype(BF16), jnp.concatenate([beta_rows, gc_rows], axis=0), NT_DIMS)

    sls = [slice(h * HEAD_DIM, (h + 1) * HEAD_DIM) for h in heads]
    q = [_l2norm(cq[:, sl]) * (HEAD_DIM ** -0.5) for sl in sls]
    k = [_l2norm(ck[:, sl]) for sl in sls]
    v = [cv[:, sl] for sl in sls]
    beta = [cols[:, h:h + 1] for h in heads]
    gcc = [cols[:, n_heads + h:n_heads + h + 1] for h in heads]
    dec_incl = [jnp.exp(jnp.where(incl, gcc[h] - gc_rows[h:h + 1, :], -jnp.inf)) for h in heads]
    egc = [jnp.exp(g) for g in gcc]
    g_last = []
    g_last_col = []
    for h in heads:
        lasts = [gcc[h][(i + 1) * ch - 1:(i + 1) * ch, :] for i in range(n_sub)]
        col = lasts[n_sub - 1]
        for i in reversed(range(n_sub - 1)):
            col = jnp.where(row_chunk == i, lasts[i], col)
        g_last.append(lasts)
        g_last_col.append(col)
    k_dec = [k[h] * jnp.exp(g_last_col[h] - gcc[h]) for h in heads]

    qk_kk = [_mm(jnp.concatenate([q[h], k[h]], axis=0), k[h], NT_DIMS) for h in heads]
    qk = [qk_kk[h][:tile] * dec_incl[h] for h in heads]
    x = [-(beta[h] * qk_kk[h][tile:] * jnp.where(strict, dec_incl[h], 0.0)) for h in heads]
    t_inv = [eye + x[h] for h in heads]
    power = 2
    while power < ch:
        x = [_mm(xh, xh) for xh in x]
        t_inv = [t_inv[h] + _mm(t_inv[h], x[h]) for h in heads]
        power *= 2
    sol = [_mm(t_inv[h], jnp.concatenate([beta[h] * v[h], (beta[h] * egc[h]) * k[h]], axis=1)) for h in heads]
    u_v = [s[:, :HEAD_DIM] for s in sol]
    w = [s[:, HEAD_DIM:] for s in sol]
    q_dec = [q[h] * egc[h] for h in heads]

    state = [s_ref[h] for h in heads]
    u_parts = [[] for _ in heads]
    qs_parts = [[] for _ in heads]
    for i in range(n_sub):
        rows = slice(i * ch, (i + 1) * ch)
        wq_s = [_mm(jnp.concatenate([w[h][rows], q_dec[h][rows]], axis=0), state[h]) for h in heads]
        u = [u_v[h][rows] - wq_s[h][:ch] for h in heads]
        state = [jnp.exp(g_last[h][i]) * state[h] + _mm(k_dec[h][rows], u[h], TN_DIMS) for h in heads]
        for h in heads:
            u_parts[h].append(u[h])
            qs_parts[h].append(wq_s[h][ch:])
    for h in heads:
        s_ref[h] = state[h]
        o = jnp.concatenate(qs_parts[h], axis=0) + _mm(qk[h], jnp.concatenate(u_parts[h], axis=0))
        y_ref[:, sls[h]] = _gated_rmsnorm(o, z_ref[:, sls[h]], norm_ref[...])

    @pl.when(c == n_tiles - 1)
    def _():
        s_out_ref[...] = s_ref[...]


def _gdn_prompt(p3, bat3, conv_w, alog_c, dtb_c, norm, *, qkv_col, z_col):
    b, t, _ = p3.shape
    n_heads = 8
    width = n_heads * HEAD_DIM
    tile = GDN_TILE
    n_tiles = t // tile
    halo_per_tile = tile // SUBLANES
    kern = functools.partial(_gdn_prompt_kernel, n_heads=n_heads, n_tiles=n_tiles)

    def rows(col):
        return pl.BlockSpec((None, tile, width), lambda bi, c: (bi, c, col))

    def halo(col):
        return pl.BlockSpec((None, SUBLANES, width),
                            lambda bi, c: (bi, jnp.maximum(c * halo_per_tile - 1, 0), col))

    def const(shape):
        return pl.BlockSpec(shape, lambda bi, c: tuple(0 for _ in shape))

    return pl.pallas_call(
        kern,
        grid=(b, n_tiles),
        in_specs=[
            rows(qkv_col), rows(qkv_col + 1), rows(qkv_col + 2),
            halo(qkv_col), halo(qkv_col + 1), halo(qkv_col + 2),
            rows(z_col),
            pl.BlockSpec((None, 2 * SUBLANES, tile), lambda bi, c: (bi * n_tiles + c, 0, 0)),
            const((CONV_W, 3 * width)),
            const((n_heads, 1)), const((n_heads, 1)),
            const((1, HEAD_DIM)),
        ],
        out_specs=[
            pl.BlockSpec((None, tile, width), lambda bi, c: (bi, c, 0)),
            pl.BlockSpec((None, n_heads, HEAD_DIM, HEAD_DIM), lambda bi, c: (bi, 0, 0, 0)),
        ],
        out_shape=[
            jax.ShapeDtypeStruct((b, t, width), F32),
            jax.ShapeDtypeStruct((b, n_heads, HEAD_DIM, HEAD_DIM), F32),
        ],
        scratch_shapes=[pltpu.VMEM((n_heads, HEAD_DIM, HEAD_DIM), F32)],
        compiler_params=_params("parallel", "arbitrary"),
        name="gdn_prompt",
    )(p3, p3, p3, p3, p3, p3, p3, bat3, conv_w, alog_c, dtb_c, norm)


def _gdn_sample_kernel(qkv_ref, conv_ref, z_ref, ba_ref, cw_ref, alog_l_ref, dtb_l_ref, norm_ref, s_ref,
                       y_ref, s_out_ref, *, n_heads):
    width3 = qkv_ref.shape[1]
    width = width3 // 3
    rows = qkv_ref.shape[0]
    w = cw_ref[...]
    y = conv_ref[:, 0:width3] * w[0:1, :]
    for i in range(1, CONV_W - 1):
        y = y + conv_ref[:, i * width3:(i + 1) * width3] * w[i:i + 1, :]
    y = y + qkv_ref[...] * w[CONV_W - 1:CONV_W, :]
    c = _silu(y)

    ba = ba_ref[...]
    beta_cols = _sigmoid(ba)
    g_cols = -jnp.exp(alog_l_ref[...]) * _softplus(ba + dtb_l_ref[...])
    decay_cols = jnp.exp(g_cols)
    eye = (lax.broadcasted_iota(jnp.int32, (HEAD_DIM, HEAD_DIM), 0)
           == lax.broadcasted_iota(jnp.int32, (HEAD_DIM, HEAD_DIM), 1)).astype(BF16)

    for h in range(n_heads):
        q = _l2norm(c[:, h * HEAD_DIM:(h + 1) * HEAD_DIM]) * (HEAD_DIM ** -0.5)
        k = _l2norm(c[:, width + h * HEAD_DIM:width + (h + 1) * HEAD_DIM])
        v = c[:, 2 * width + h * HEAD_DIM:2 * width + (h + 1) * HEAD_DIM]
        q_t = _mm_exact_lhs(eye, q, NT_DIMS)
        k_t = _mm_exact_lhs(eye, k, NT_DIMS)
        seqs = range(rows)
        k_col = [k_t[:, r:r + 1] for r in seqs]
        s = [decay_cols[r:r + 1, n_heads + h:n_heads + h + 1] * s_ref[r, h] for r in seqs]
        ks = [jnp.sum(k_col[r] * s[r], axis=0, keepdims=True) for r in seqs]
        u = [beta_cols[r:r + 1, h:h + 1] * (v[r:r + 1, :] - ks[r]) for r in seqs]
        s = [s[r] + k_col[r] * u[r] for r in seqs]
        for r in seqs:
            s_out_ref[r, h] = s[r]
        o = jnp.concatenate([jnp.sum(q_t[:, r:r + 1] * s[r], axis=0, keepdims=True) for r in seqs], axis=0)
        sl = slice(h * HEAD_DIM, (h + 1) * HEAD_DIM)
        y_ref[:, sl] = _gated_rmsnorm(o, z_ref[:, sl], norm_ref[...])


def _gdn_sample(p2, conv_prev2, ba2, states, layer, conv_w, alog_l, dtb_l, norm, *, qkv_col, z_col):
    db = p2.shape[0]
    n_heads = states.shape[2]
    width = n_heads * HEAD_DIM
    rows = SUBLANES
    kern = functools.partial(_gdn_sample_kernel, n_heads=n_heads)

    def const(shape):
        return pl.BlockSpec(shape, lambda i: tuple(0 for _ in shape))

    return pl.pallas_call(
        kern,
        grid=(db // rows,),
        in_specs=[
            pl.BlockSpec((rows, 3 * width), lambda i: (i, qkv_col // 3)),
            pl.BlockSpec((rows, (CONV_W - 1) * 3 * width), lambda i: (i, 0)),
            pl.BlockSpec((rows, width), lambda i: (i, z_col)),
            pl.BlockSpec((rows, HEAD_DIM), lambda i: (i, 0)),
            const((CONV_W, 3 * width)),
            const((1, HEAD_DIM)), const((1, HEAD_DIM)), const((1, HEAD_DIM)),
            pl.BlockSpec((None, rows, n_heads, HEAD_DIM, HEAD_DIM), lambda i: (layer, i, 0, 0, 0)),
        ],
        out_specs=[
            pl.BlockSpec((rows, width), lambda i: (i, 0)),
            pl.BlockSpec((rows, n_heads, HEAD_DIM, HEAD_DIM), lambda i: (i, 0, 0, 0)),
        ],
        out_shape=[
            jax.ShapeDtypeStruct((db, width), F32),
            jax.ShapeDtypeStruct(states.shape[1:], F32),
        ],
        compiler_params=_params("parallel"),
        name="gdn_sample",
    )(p2, conv_prev2, p2, ba2, conv_w, alog_l, dtb_l, norm, states)


def _merge_kernel(x_ref, yg_ref, ym_ref, sg_ref, sm_ref, wpg_ref, wpm_ref, wo_ref, o_ref):
    yg = jnp.dot(yg_ref[...].astype(BF16), wpg_ref[...], preferred_element_type=F32)
    ym = jnp.dot(ym_ref[...].astype(BF16), wpm_ref[...], preferred_element_type=F32)
    mixed = sg_ref[...] * yg + sm_ref[...] * ym
    o_ref[...] = x_ref[...] + jnp.dot(mixed.astype(BF16), wo_ref[...], preferred_element_type=F32)


def _merge(x, yg, ym, p, w_pg, w_pm, w_o, layer, *, tm, sg_col, sm_col):
    m, d = x.shape

    def rows(col):
        return pl.BlockSpec((tm, d), lambda i: (i, col))

    def weight():
        return pl.BlockSpec((None, d, d), lambda i: (layer, 0, 0))

    return pl.pallas_call(
        _merge_kernel,
        grid=(m // tm,),
        in_specs=[rows(0), rows(0), rows(0), rows(sg_col), rows(sm_col), weight(), weight(), weight()],
        out_specs=rows(0),
        out_shape=jax.ShapeDtypeStruct((m, d), F32),
        compiler_params=_params("parallel"),
        name="merge",
    )(x, yg, ym, p, p, w_pg, w_pm, w_o)


def _ffn_kernel(x_ref, ln_ref, wg_ref, wu_ref, wd_ref, lnf_ref, o_ref, xn_ref, acc_ref, *, final):
    j = pl.program_id(1)

    @pl.when(j == 0)
    def _():
        x = x_ref[...]
        y = x * lax.rsqrt(jnp.mean(x * x, axis=-1, keepdims=True) + EPS)
        xn_ref[...] = (y * ln_ref[...]).astype(BF16)
        acc_ref[...] = jnp.zeros_like(acc_ref)

    xn = xn_ref[...]
    g = jnp.dot(xn, wg_ref[...], preferred_element_type=F32)
    u = jnp.dot(xn, wu_ref[...], preferred_element_type=F32)
    acc_ref[...] += jnp.dot((_silu(g) * u).astype(BF16), wd_ref[...], preferred_element_type=F32)

    @pl.when(j == pl.num_programs(1) - 1)
    def _():
        out = x_ref[...] + acc_ref[...]
        if final:
            out = out * lax.rsqrt(jnp.mean(out * out, axis=-1, keepdims=True) + EPS) * lnf_ref[...]
        o_ref[...] = out


def _ffn(x, ln, w_gu, w_dn, ln_final, layer, *, tm, tf, final):
    m, d = x.shape
    d_ff = w_dn.shape[1]
    n_f = d_ff // tf
    return pl.pallas_call(
        functools.partial(_ffn_kernel, final=final),
        grid=(m // tm, n_f),
        in_specs=[
            pl.BlockSpec((tm, d), lambda i, j: (i, 0)),
            pl.BlockSpec((1, d), lambda i, j: (0, 0)),
            pl.BlockSpec((None, d, tf), lambda i, j: (layer, 0, j)),
            pl.BlockSpec((None, d, tf), lambda i, j: (layer, 0, n_f + j)),
            pl.BlockSpec((None, tf, d), lambda i, j: (layer, j, 0)),
            pl.BlockSpec((1, d), lambda i, j: (0, 0)),
        ],
        out_specs=pl.BlockSpec((tm, d), lambda i, j: (i, 0)),
        out_shape=jax.ShapeDtypeStruct((m, d), F32),
        scratch_shapes=[pltpu.VMEM((tm, d), BF16), pltpu.VMEM((tm, d), F32)],
        compiler_params=_params("parallel", "arbitrary"),
        name="ffn",
    )(x, ln, w_gu, w_gu, w_dn, ln_final)


def _rope_tables(pos):
    half = HEAD_DIM // 2
    inv = jnp.exp(-math.log(ROPE_THETA) * jnp.arange(half, dtype=F32) / half)
    ang = pos.astype(F32)[:, None] * inv[None, :]
    cos = jnp.cos(ang)
    sin = jnp.sin(ang)
    return jnp.concatenate([cos, cos], axis=-1), jnp.concatenate([-sin, sin], axis=-1)


def _lane_row(vec, offset):
    return jnp.zeros((1, HEAD_DIM), F32).at[0, offset:offset + vec.shape[0]].set(vec.astype(F32))


def kernel(x_prompt, x_sample, cache_k, cache_v, state_ssm, state_conv, page_table, ln_mix, ln_ffn, w_in,
           conv_w, a_log, dt_bias, gdn_norm, w_branch_gdn, w_branch_moba, w_out, w_gate_up, w_down, ln_final):
    b, t, d = x_prompt.shape
    db, ts, _ = x_sample.shape
    depth = w_in.shape[0]
    n_heads = d // HEAD_DIM
    width = n_heads * HEAD_DIM
    n_pages = page_table.shape[1]
    past = n_pages * PAGE_SIZE
    d_ff = w_down.shape[1]
    assert ts == 1 and n_heads == 8 and width == d
    assert t % MOBA_BLOCK == 0 and t % 1024 == 0 and db % SUBLANES == 0

    c_ba = 7 * width
    c_g = c_ba + 2 * n_heads
    col_q, col_k, col_v, col_qkv, col_z, col_sg, col_sm = 0, 1, 2, 3, 6, 7, 8

    cos_p, sin_p = _rope_tables(jnp.arange(t, dtype=jnp.int32))
    cos_s, sin_s = _rope_tables(jnp.full((db,), past, dtype=jnp.int32))

    tm_p = 1024
    tf = d_ff // 2 if (d_ff // 2) % HEAD_DIM == 0 else d_ff
    xp = x_prompt.reshape(b * t, d)
    xs = x_sample.reshape(db * ts, d)
    outs = {name: [] for name in ("kp", "vp", "sp", "cp", "ks", "vs", "ss", "cs")}
    w_all = w_in.astype(BF16)
    w_gates = w_in[:, :, c_g:].astype(BF16)
    w_ba = jnp.pad(w_in[:, :, c_ba:c_g], ((0, 0), (0, 0), (0, HEAD_DIM - 2 * n_heads))).astype(BF16)
    w_bat = jnp.swapaxes(w_in[:, :, c_ba:c_g], 1, 2).astype(BF16)
    w_pg = w_branch_gdn.astype(BF16)
    w_pm = w_branch_moba.astype(BF16)
    w_o = w_out.astype(BF16)
    w_gu = w_gate_up.astype(BF16)
    w_dn = w_down.astype(BF16)
    for l in range(depth):
        last = l == depth - 1
        ln_m = ln_mix[l].reshape(1, d)
        ln_f = ln_ffn[l].reshape(1, d)
        alog_l = _lane_row(a_log[l], n_heads)
        dtb_l = _lane_row(dt_bias[l], n_heads)
        alog_c = a_log[l].reshape(n_heads, 1).astype(F32)
        dtb_c = dt_bias[l].reshape(n_heads, 1).astype(F32)
        norm = gdn_norm[l].reshape(1, HEAD_DIM)
        proj = functools.partial(_in_proj, tn=width, n_rope_tiles=2, n_lead_tiles=col_sg, k_tile=col_k, v_tile=col_v)

        p, ba, bat, k_new, v_new = proj(xp, ln_m, w_all, w_gates, w_ba, w_bat, cos_p, sin_p, l, tm=tm_p)
        p3 = p.reshape(b, t, -1)
        ym = _moba_prompt(p3, q_col=col_q * n_heads, k_col=col_k * n_heads, v_col=col_v * n_heads)
        bat3 = bat.reshape(2 * n_heads, b * t // GDN_TILE, GDN_TILE).transpose(1, 0, 2)
        yg, s_fin = _gdn_prompt(p3, bat3, conv_w[l], alog_c, dtb_c, norm, qkv_col=col_qkv, z_col=col_z)
        x1 = _merge(xp, yg.reshape(b * t, width), ym.reshape(b * t, width), p, w_pg, w_pm, w_o, l,
                    tm=tm_p // 2, sg_col=col_sg, sm_col=col_sm)
        xp = _ffn(x1, ln_f, w_gu, w_dn, ln_final.reshape(1, d), l, tm=tm_p, tf=tf, final=last)
        outs["kp"].append(k_new)
        outs["vp"].append(v_new)
        outs["sp"].append(s_fin)
        outs["cp"].append(p3[:, t - (CONV_W - 1):, col_qkv * width:(col_qkv + 3) * width])

        p, ba, bat, k_new, v_new = proj(xs, ln_m, w_all, w_gates, w_ba, w_bat, cos_s, sin_s, l, tm=db)
        ph = p.reshape(db, -1, HEAD_DIM)
        ym = _moba_sample(ph, cache_k, cache_v, page_table, l, q_row=col_q, k_row=col_k, v_row=col_v)
        yg, s_new = _gdn_sample(p, state_conv[l].reshape(db, -1), ba, state_ssm, l, conv_w[l], alog_l, dtb_l,
                                norm, qkv_col=col_qkv, z_col=col_z)
        x1 = _merge(xs, yg, ym.reshape(db, width), p, w_pg, w_pm, w_o, l, tm=db, sg_col=col_sg, sm_col=col_sm)
        xs = _ffn(x1, ln_f, w_gu, w_dn, ln_final.reshape(1, d), l, tm=db, tf=tf, final=last)
        qkv_new = p[:, col_qkv * width:(col_qkv + 3) * width]
        outs["ks"].append(k_new)
        outs["vs"].append(v_new)
        outs["ss"].append(s_new)
        outs["cs"].append(jnp.concatenate([state_conv[l][:, 1:], qkv_new[:, None, :]], axis=1))

    n_tpages = t // PAGE_SIZE
    return (
        xp.reshape(b, t, d),
        xs.reshape(db, ts, d),
        jnp.stack(outs["kp"]).reshape(depth, b, n_tpages, PAGE_SIZE, n_heads, HEAD_DIM),
        jnp.stack(outs["vp"]).reshape(depth, b, n_tpages, PAGE_SIZE, n_heads, HEAD_DIM),
        jnp.stack(outs["sp"]),
        jnp.stack(outs["cp"]),
        jnp.stack(outs["ks"]).reshape(depth, db, ts, n_heads, HEAD_DIM),
        jnp.stack(outs["vs"]).reshape(depth, db, ts, n_heads, HEAD_DIM),
        jnp.stack(outs["ss"]),
        jnp.stack(outs["cs"]),
    )
```

```python
import functools
import math

import jax
import jax.numpy as jnp
import numpy as np
from jax import lax
from jax.experimental import pallas as pl
from jax.experimental.pallas import tpu as pltpu

F32 = jnp.float32
BF16 = jnp.bfloat16

HEAD_DIM = 128
MOBA_BLOCK = 256
MOBA_TOPK = 3
MOBA_PAIR_UNROLL = 4
GDN_CHUNK = 64
GDN_TILE = 2 * GDN_CHUNK
CONV_W = 4
PAGE_SIZE = 128
ROPE_THETA = 10000.0
EPS = 1e-6
SUBLANES = 8
VMEM_LIMIT_BYTES = 50 * 1024 * 1024
LOG2_E = 1.4426950408889634

NT_DIMS = (((1,), (1,)), ((), ()))
TN_DIMS = (((0,), (0,)), ((), ()))


def _params(*semantics):
    return pltpu.CompilerParams(dimension_semantics=semantics, vmem_limit_bytes=VMEM_LIMIT_BYTES)


def _mm(a, b, dims=None):
    a = a.astype(BF16)
    b = b.astype(BF16)
    if dims is None:
        return jnp.dot(a, b, preferred_element_type=F32)
    return lax.dot_general(a, b, dims, preferred_element_type=F32)


def _split3(x):
    hi = x.astype(BF16)
    r1 = x - hi.astype(F32)
    mid = r1.astype(BF16)
    lo = (r1 - mid.astype(F32)).astype(BF16)
    return hi, mid, lo


def _mm_exact_rhs(a, b_exact, dims=None):
    out = None
    for part in _split3(a):
        t = _mm(part, b_exact, dims)
        out = t if out is None else out + t
    return out


def _mm_exact_lhs(a_exact, b, dims=None):
    out = None
    for part in _split3(b):
        t = _mm(a_exact, part, dims)
        out = t if out is None else out + t
    return out


def _sigmoid(x):
    return 1.0 / (1.0 + jnp.exp(-x))


def _silu(x):
    return x * _sigmoid(x)


def _softplus(x):
    return jnp.maximum(x, 0.0) + jnp.log1p(jnp.exp(-jnp.abs(x)))


def _in_proj_kernel(x_ref, ln_ref, wa_ref, wg_ref, wba_ref, wbat_ref, cos_ref, sin_ref,
                    p_ref, ba_ref, bat_ref, k_ref, v_ref, xn_ref, *, n_rope_tiles, n_lead_tiles, k_tile, v_tile):
    j = pl.program_id(1)

    @pl.when(j == 0)
    def _():
        x = x_ref[...]
        y = x * lax.rsqrt(jnp.mean(x * x, axis=-1, keepdims=True) + EPS)
        xn = (y * ln_ref[...]).astype(BF16)
        xn_ref[...] = xn
        ba_ref[...] = jnp.dot(xn, wba_ref[...], preferred_element_type=F32)
        bat_ref[...] = lax.dot_general(wbat_ref[...], xn, NT_DIMS, preferred_element_type=F32)

    @pl.when(j < n_rope_tiles)
    def _():
        acc = jnp.dot(xn_ref[...], wa_ref[...], preferred_element_type=F32)
        cos = cos_ref[...]
        sin = sin_ref[...]
        for h in range(acc.shape[1] // HEAD_DIM):
            sl = slice(h * HEAD_DIM, (h + 1) * HEAD_DIM)
            xh = acc[:, sl]
            p_ref[:, sl] = xh * cos + pltpu.roll(xh, HEAD_DIM // 2, 1) * sin

        @pl.when(j == k_tile)
        def _():
            k_ref[...] = p_ref[...]

    @pl.when((j >= n_rope_tiles) & (j < n_lead_tiles))
    def _():
        p_ref[...] = jnp.dot(xn_ref[...], wa_ref[...], preferred_element_type=F32)

        @pl.when(j == v_tile)
        def _():
            v_ref[...] = p_ref[...]

    @pl.when(j >= n_lead_tiles)
    def _():
        p_ref[...] = _sigmoid(jnp.dot(xn_ref[...], wg_ref[...], preferred_element_type=F32))


def _in_proj(x, ln, w_all, w_gates, w_ba, w_bat, cos, sin, layer, *, tm, tn, n_rope_tiles, n_lead_tiles,
             k_tile, v_tile):
    m, d = x.shape
    n_tiles = n_lead_tiles + w_gates.shape[2] // tn
    rope_blocks = cos.shape[0] // tm
    assert k_tile < n_rope_tiles <= v_tile < n_lead_tiles
    kern = functools.partial(_in_proj_kernel, n_rope_tiles=n_rope_tiles, n_lead_tiles=n_lead_tiles,
                             k_tile=k_tile, v_tile=v_tile)
    n = n_tiles * tn
    return pl.pallas_call(
        kern,
        grid=(m // tm, n_tiles),
        in_specs=[
            pl.BlockSpec((tm, d), lambda i, j: (i, 0)),
            pl.BlockSpec((1, d), lambda i, j: (0, 0)),
            pl.BlockSpec((None, d, tn), lambda i, j: (layer, 0, jnp.minimum(j, n_lead_tiles - 1))),
            pl.BlockSpec((None, d, tn), lambda i, j: (layer, 0, jnp.maximum(j - n_lead_tiles, 0))),
            pl.BlockSpec((None, d, HEAD_DIM), lambda i, j: (layer, 0, 0)),
            pl.BlockSpec((None, 2 * SUBLANES, d), lambda i, j: (layer, 0, 0)),
            pl.BlockSpec((tm, HEAD_DIM), lambda i, j: (i % rope_blocks, 0)),
            pl.BlockSpec((tm, HEAD_DIM), lambda i, j: (i % rope_blocks, 0)),
        ],
        out_specs=[
            pl.BlockSpec((tm, tn), lambda i, j: (i, j)),
            pl.BlockSpec((tm, HEAD_DIM), lambda i, j: (i, 0)),
            pl.BlockSpec((2 * SUBLANES, tm), lambda i, j: (0, i)),
            pl.BlockSpec((tm, tn), lambda i, j: (i, 0)),
            pl.BlockSpec((tm, tn), lambda i, j: (i, 0)),
        ],
        out_shape=[
            jax.ShapeDtypeStruct((m, n), F32),
            jax.ShapeDtypeStruct((m, HEAD_DIM), F32),
            jax.ShapeDtypeStruct((2 * SUBLANES, m), F32),
            jax.ShapeDtypeStruct((m, tn), F32),
            jax.ShapeDtypeStruct((m, tn), F32),
        ],
        scratch_shapes=[pltpu.VMEM((tm, d), BF16)],
        compiler_params=_params("parallel", "arbitrary"),
        name="in_proj",
    )(x, ln, w_all, w_gates, w_ba, w_bat, cos, sin)


def _top_blocks_cols(gate, n_valid, n_top):
    n, cols = gate.shape
    blk = lax.broadcasted_iota(jnp.int32, (n, cols), 0).astype(F32)
    g = jnp.where(blk < n_valid, gate, -jnp.inf)
    picks = []
    for _ in range(n_top):
        m = jnp.max(g, axis=0, keepdims=True)
        cand = jnp.where((g == m) & (m > -jnp.inf), blk, float(n))
        idx = jnp.min(cand, axis=0, keepdims=True)
        picks.append(idx)
        g = jnp.where(blk == idx, -jnp.inf, g)
    return picks


def _moba_prompt_kernel(pj_ref, pq_ref, q_ref, k_ref, v_ref, o_ref,
                        kb_ref, vt_ref, qt_ref, kmean_ref, picks_ref, m_ref, l_ref, acc_ref, sbuf_ref,
                        *, n_blocks, n_groups, unroll):
    blk = MOBA_BLOCK
    n_top = min(MOBA_TOPK, n_blocks)
    score_scale = (HEAD_DIM ** -0.5) * LOG2_E

    def block_rows(j):
        return pl.ds(pl.multiple_of(j * blk, blk), blk)

    def prepare_block(j, carry):
        kj = k_ref[block_rows(j), :]
        kb_ref[block_rows(j), :] = kj.astype(BF16)
        vt_ref[j] = v_ref[block_rows(j), :].T.astype(BF16)
        kmean_ref[pl.ds(j, 1), :] = jnp.mean(kj, axis=0, keepdims=True)
        return carry

    lax.fori_loop(0, n_blocks, prepare_block, 0)

    key = lax.broadcasted_iota(jnp.int32, (blk, blk), 0)
    qry = lax.broadcasted_iota(jnp.int32, (blk, blk), 1)
    no_pick = jnp.full((SUBLANES - n_top, blk), -1.0, F32)

    def init_tile(qb):
        q_t = q_ref[block_rows(qb), :].T
        gate = jnp.dot(kmean_ref[...], q_t, preferred_element_type=F32, precision=lax.Precision.HIGHEST)
        picks = _top_blocks_cols(gate, lax.convert_element_type(qb, F32), n_top)
        picks_ref[qb] = jnp.concatenate(picks + [no_pick], axis=0)
        q16 = (q_t * score_scale).astype(BF16)
        qt_ref[qb] = q16
        s = jnp.dot(kb_ref[block_rows(qb), :], q16, preferred_element_type=F32)
        s = jnp.where(key <= qry, s, -jnp.inf)
        m = jnp.max(s, axis=0, keepdims=True)
        p = jnp.exp2(s - m)
        m_ref[qb] = m
        l_ref[qb] = jnp.sum(p, axis=0, keepdims=True)
        acc_ref[qb] = jnp.dot(vt_ref[qb], p.astype(BF16), preferred_element_type=F32)

    def init_tiles(i, carry):
        for u in range(unroll):
            init_tile(i * unroll + u)
        return carry

    lax.fori_loop(0, n_blocks // unroll, init_tiles, 0)

    picks_ref[n_blocks] = jnp.full((SUBLANES, blk), -1.0, F32)
    qt_ref[n_blocks] = jnp.zeros((HEAD_DIM, blk), BF16)
    m_ref[n_blocks] = jnp.zeros((1, blk), F32)
    l_ref[n_blocks] = jnp.zeros((1, blk), F32)
    acc_ref[n_blocks] = jnp.zeros((HEAD_DIM, blk), F32)

    def scores(g, slot):
        for u in range(unroll):
            ju = pj_ref[g * unroll + u]
            qu = pq_ref[g * unroll + u]
            sbuf_ref[slot, u] = jnp.dot(kb_ref[block_rows(ju), :], qt_ref[qu], preferred_element_type=F32)

    scores(0, 0)

    def pair_group(g, slot):
        lanes = range(unroll)
        js = [pj_ref[g * unroll + u] for u in lanes]
        qs = [pq_ref[g * unroll + u] for u in lanes]
        scores(jnp.minimum(g + 1, n_groups - 1), 1 - slot)
        m_old = [m_ref[qs[u]] for u in lanes]
        l_old = [l_ref[qs[u]] for u in lanes]
        acc_old = [acc_ref[qs[u]] for u in lanes]
        picks = [picks_ref[qs[u]] for u in lanes]
        s = [sbuf_ref[slot, u] for u in lanes]
        sel = []
        for u in lanes:
            jf = lax.convert_element_type(js[u], F32)
            hit = picks[u][0:1, :] == jf
            for t in range(1, n_top):
                hit = hit | (picks[u][t:t + 1, :] == jf)
            sel.append(hit)
        m_blk = [jnp.where(sel[u], jnp.max(s[u], axis=0, keepdims=True), -jnp.inf) for u in lanes]
        m_new = [jnp.maximum(m_old[u], m_blk[u]) for u in lanes]
        alpha = [jnp.exp2(m_old[u] - m_new[u]) for u in lanes]
        shift = [jnp.where(sel[u], m_new[u], jnp.inf) for u in lanes]
        p = [jnp.exp2(s[u] - shift[u]) for u in lanes]
        l_new = [alpha[u] * l_old[u] + jnp.sum(p[u], axis=0, keepdims=True) for u in lanes]
        acc_new = [alpha[u] * acc_old[u]
                   + jnp.dot(vt_ref[js[u]], p[u].astype(BF16), preferred_element_type=F32) for u in lanes]
        for u in lanes:
            m_ref[qs[u]] = m_new[u]
            l_ref[qs[u]] = l_new[u]
            acc_ref[qs[u]] = acc_new[u]

    def two_groups(i, carry):
        pair_group(2 * i, 0)
        pair_group(2 * i + 1, 1)
        return carry

    lax.fori_loop(0, n_groups // 2, two_groups, 0)

    def finish_tile(qb, carry):
        o_ref[block_rows(qb), :] = (acc_ref[qb] / l_ref[qb]).T
        return carry

    lax.fori_loop(0, n_blocks, finish_tile, 0)


def _moba_pair_table(n_blocks, unroll):
    todo = {q: list(range(q)) for q in range(1, n_blocks)}
    pj, pq = [], []
    while todo:
        tiles = sorted(todo, key=lambda q: -len(todo[q]))[:unroll]
        for q in tiles:
            pj.append(todo[q].pop())
            pq.append(q)
            if not todo[q]:
                del todo[q]
        pj += [0] * (unroll - len(tiles))
        pq += [n_blocks] * (unroll - len(tiles))
    if not pj:
        pj, pq = [0] * unroll, [n_blocks] * unroll
    if (len(pj) // unroll) % 2:
        pj += [0] * unroll
        pq += [n_blocks] * unroll
    return np.asarray(pj, np.int32), np.asarray(pq, np.int32)


def _moba_prompt(p3, *, q_col, k_col, v_col):
    b, t, _ = p3.shape
    n_heads = 8
    nb = t // MOBA_BLOCK
    unroll = MOBA_PAIR_UNROLL if nb % MOBA_PAIR_UNROLL == 0 else 1
    pair_j, pair_q = _moba_pair_table(nb, unroll)
    kern = functools.partial(_moba_prompt_kernel, n_blocks=nb, n_groups=len(pair_j) // unroll, unroll=unroll)

    def seq(col):
        return pl.BlockSpec((None, t, HEAD_DIM), lambda bi, h, pj, pq: (bi, 0, col + h))

    grid_spec = pltpu.PrefetchScalarGridSpec(
        num_scalar_prefetch=2,
        grid=(b, n_heads),
        in_specs=[seq(q_col), seq(k_col), seq(v_col)],
        out_specs=seq(0),
        scratch_shapes=[
            pltpu.VMEM((t, HEAD_DIM), BF16),
            pltpu.VMEM((nb, HEAD_DIM, MOBA_BLOCK), BF16),
            pltpu.VMEM((nb + 1, HEAD_DIM, MOBA_BLOCK), BF16),
            pltpu.VMEM((nb, HEAD_DIM), F32),
            pltpu.VMEM((nb + 1, SUBLANES, MOBA_BLOCK), F32),
            pltpu.VMEM((nb + 1, 1, MOBA_BLOCK), F32),
            pltpu.VMEM((nb + 1, 1, MOBA_BLOCK), F32),
            pltpu.VMEM((nb + 1, HEAD_DIM, MOBA_BLOCK), F32),
            pltpu.VMEM((2, unroll, MOBA_BLOCK, MOBA_BLOCK), F32),
        ],
    )
    return pl.pallas_call(
        kern,
        grid_spec=grid_spec,
        out_shape=jax.ShapeDtypeStruct((b, t, n_heads * HEAD_DIM), F32),
        compiler_params=_params("parallel", "parallel"),
        name="moba_prompt",
    )(jnp.asarray(pair_j), jnp.asarray(pair_q), p3, p3, p3)


def _moba_sample_kernel(pt_ref, q_ref, kn_ref, vn_ref, *refs, n_pages):
    del pt_ref
    k_refs = refs[:n_pages]
    v_refs = refs[n_pages:2 * n_pages]
    o_ref = refs[2 * n_pages]
    scale = HEAD_DIM ** -0.5
    pages_per_block = MOBA_BLOCK // PAGE_SIZE
    n_past = n_pages // pages_per_block
    q = q_ref[...]

    gates, ms, ls, accs = [], [], [], []
    for j in range(n_past):
        pages = range(j * pages_per_block, (j + 1) * pages_per_block)
        ks = [k_refs[pg][...] for pg in pages]
        ss = [jnp.sum(k * q[None], axis=-1, keepdims=True) * scale for k in ks]
        ksum = ks[0].sum(axis=0)
        m = jnp.max(ss[0], axis=0)
        for k, s in zip(ks[1:], ss[1:]):
            ksum = ksum + k.sum(axis=0)
            m = jnp.maximum(m, jnp.max(s, axis=0))
        l = None
        acc = None
        for pg, s in zip(pages, ss):
            p = jnp.exp(s - m[None])
            pl_sum = p.sum(axis=0)
            pv = (p * v_refs[pg][...]).sum(axis=0)
            l = pl_sum if l is None else l + pl_sum
            acc = pv if acc is None else acc + pv
        gates.append(jnp.sum(q * (ksum * (1.0 / MOBA_BLOCK)), axis=-1, keepdims=True))
        ms.append(m)
        ls.append(l)
        accs.append(acc)

    n_top = min(MOBA_TOPK, n_past + 1)
    remaining = list(gates)
    selected = [jnp.zeros_like(gates[0], dtype=jnp.bool_) for _ in range(n_past)]
    for _ in range(n_top):
        best = remaining[0]
        for g in remaining[1:]:
            best = jnp.maximum(best, g)
        idx = jnp.full_like(best, float(n_past))
        for j in reversed(range(n_past)):
            idx = jnp.where(remaining[j] == best, float(j), idx)
        idx = jnp.where(best > -jnp.inf, idx, float(n_past))
        for j in range(n_past):
            hit = idx == float(j)
            selected[j] = selected[j] | hit
            remaining[j] = jnp.where(hit, -jnp.inf, remaining[j])

    s_own = jnp.sum(q * kn_ref[...], axis=-1, keepdims=True) * scale
    m_tot = s_own
    for j in range(n_past):
        m_tot = jnp.maximum(m_tot, jnp.where(selected[j], ms[j], -jnp.inf))
    e_own = jnp.exp(s_own - m_tot)
    l_tot = e_own
    o = e_own * vn_ref[...]
    for j in range(n_past):
        w = jnp.where(selected[j], jnp.exp(ms[j] - m_tot), 0.0)
        l_tot = l_tot + w * ls[j]
        o = o + w * accs[j]
    o_ref[...] = o / l_tot


def _moba_sample(ph, cache_k, cache_v, page_table, layer, *, q_row, k_row, v_row):
    db = ph.shape[0]
    n_pages = page_table.shape[1]
    n_heads = cache_k.shape[3]
    assert MOBA_BLOCK % PAGE_SIZE == 0 and n_pages % (MOBA_BLOCK // PAGE_SIZE) == 0
    assert n_heads == SUBLANES
    kern = functools.partial(_moba_sample_kernel, n_pages=n_pages)

    def row_spec(row):
        return pl.BlockSpec((None, n_heads, HEAD_DIM), lambda bi, pt: (bi, row, 0))

    def page_spec(page):
        return pl.BlockSpec((None, None, PAGE_SIZE, n_heads, HEAD_DIM),
                            lambda bi, pt: (layer, pt[bi * n_pages + page], 0, 0, 0))

    grid_spec = pltpu.PrefetchScalarGridSpec(
        num_scalar_prefetch=1,
        grid=(db,),
        in_specs=([row_spec(q_row), row_spec(k_row), row_spec(v_row)]
                  + [page_spec(pg) for pg in range(n_pages)] * 2),
        out_specs=pl.BlockSpec((None, n_heads, HEAD_DIM), lambda bi, pt: (bi, 0, 0)),
    )
    return pl.pallas_call(
        kern,
        grid_spec=grid_spec,
        out_shape=jax.ShapeDtypeStruct((db, n_heads, HEAD_DIM), F32),
        compiler_params=_params("parallel"),
        name="moba_sample",
    )(page_table.reshape(-1), ph, ph, ph, *([cache_k] * n_pages), *([cache_v] * n_pages))


def _l2norm(x):
    return x * lax.rsqrt(jnp.sum(x * x, axis=-1, keepdims=True) + 1e-6)


def _gated_rmsnorm(o, z, w):
    on = o * lax.rsqrt(jnp.mean(o * o, axis=-1, keepdims=True) + EPS) * w
    return on * _silu(z)


def _gdn_prompt_kernel(q_ref, k_ref, v_ref, qh_ref, kh_ref, vh_ref, z_ref, bat_ref,
                       cw_ref, alog_c_ref, dtb_c_ref, norm_ref,
                       y_ref, s_out_ref, s_ref, *, n_heads, n_tiles):
    c = pl.program_id(1)
    ch = GDN_CHUNK
    tile = GDN_TILE
    n_sub = tile // ch
    heads = range(n_heads)

    @pl.when(c == 0)
    def _():
        s_ref[...] = jnp.zeros_like(s_ref)

    def conv(u_ref, halo_ref, part):
        halo = jnp.where(c == 0, 0.0, halo_ref[...])
        ext = jnp.concatenate([halo, u_ref[...]], axis=0)
        w = cw_ref[:, part * n_heads * HEAD_DIM:(part + 1) * n_heads * HEAD_DIM]
        y = None
        for i in range(CONV_W):
            shift = CONV_W - 1 - i
            tap = ext if shift == 0 else pltpu.roll(ext, shift, 0)
            term = tap[SUBLANES:, :] * w[i:i + 1, :]
            y = term if y is None else y + term
        return _silu(y)

    cq = conv(q_ref, qh_ref, 0)
    ck = conv(k_ref, kh_ref, 1)
    cv = conv(v_ref, vh_ref, 2)

    ri = lax.broadcasted_iota(jnp.int32, (tile, tile), 0)
    ci = lax.broadcasted_iota(jnp.int32, (tile, tile), 1)
    chunk_shift = ch.bit_length() - 1
    same_chunk = jnp.right_shift(ri, chunk_shift) == jnp.right_shift(ci, chunk_shift)
    incl = same_chunk & (ri >= ci)
    strict = same_chunk & (ri > ci)
    eye = (ri == ci).astype(F32)
    row_chunk = jnp.right_shift(lax.broadcasted_iota(jnp.int32, (tile, 1), 0), chunk_shift)

    beta_rows = _sigmoid(bat_ref[:n_heads, :])
    g_rows = -jnp.exp(alog_c_ref[...]) * _softplus(bat_ref[n_heads:, :] + dtb_c_ref[...])
    gc_rows = _mm_exact_rhs(g_rows, (same_chunk & (ri <= ci)).astype(BF16))
    cols = _mm_exact_lhs(eye.astype(BF16), jnp.concatenate([beta_rows, gc_rows], axis=0), NT_DIMS)

    sls = [slice(h * HEAD_DIM, (h + 1) * HEAD_DIM) for h in heads]
    q = [_l2norm(cq[:, sl]) * (HEAD_DIM ** -0.5) for sl in sls]
    k = [_l2norm(ck[:, sl]) for sl in sls]
    v = [cv[:, sl] for sl in sls]
    beta = [cols[:, h:h + 1] for h in heads]
    gcc = [cols[:, n_heads + h:n_heads + h + 1] for h in heads]
    dec_incl = [jnp.exp(jnp.where(incl, gcc[h] - gc_rows[h:h + 1, :], -jnp.inf)) for h in heads]
    egc = [jnp.exp(g) for g in gcc]
    g_last = []
    g_last_col = []
    for h in heads:
        lasts = [gcc[h][(i + 1) * ch - 1:(i + 1) * ch, :] for i in range(n_sub)]
        col = lasts[n_sub - 1]
        for i in reversed(range(n_sub - 1)):
            col = jnp.where(row_chunk == i, lasts[i], col)
        g_last.append(lasts)
        g_last_col.append(col)
    k_dec = [k[h] * jnp.exp(g_last_col[h] - gcc[h]) for h in heads]

    qk_kk = [_mm(jnp.concatenate([q[h], k[h]], axis=0), k[h], NT_DIMS) for h in heads]
    qk = [qk_kk[h][:tile] * dec_incl[h] for h in heads]
    x = [-(beta[h] * qk_kk[h][tile:] * jnp.where(strict, dec_incl[h], 0.0)) for h in heads]
    t_inv = [eye + x[h] for h in heads]
    power = 2
    while power < ch:
        x = [_mm(xh, xh) for xh in x]
        t_inv = [t_inv[h] + _mm(t_inv[h], x[h]) for h in heads]
        power *= 2
    sol = [_mm(t_inv[h], jnp.concatenate([beta[h] * v[h], (beta[h] * egc[h]) * k[h]], axis=1)) for h in heads]
    u_v = [s[:, :HEAD_DIM] for s in sol]
    w = [s[:, HEAD_DIM:] for s in sol]
    q_dec = [q[h] * egc[h] for h in heads]

    state = [s_ref[h] for h in heads]
    u_parts = [[] for _ in heads]
    qs_parts = [[] for _ in heads]
    for i in range(n_sub):
        rows = slice(i * ch, (i + 1) * ch)
        wq_s = [_mm(jnp.concatenate([w[h][rows], q_dec[h][rows]], axis=0), state[h]) for h in heads]
        u = [u_v[h][rows] - wq_s[h][:ch] for h in heads]
        state = [jnp.exp(g_last[h][i]) * state[h] + _mm(k_dec[h][rows], u[h], TN_DIMS) for h in heads]
        for h in heads:
            u_parts[h].append(u[h])
            qs_parts[h].append(wq_s[h][ch:])
    for h in heads:
        s_ref[h] = state[h]
        o = jnp.concatenate(qs_parts[h], axis=0) + _mm(qk[h], jnp.concatenate(u_parts[h], axis=0))
        y_ref[:, sls[h]] = _gated_rmsnorm(o, z_ref[:, sls[h]], norm_ref[...])

    @pl.when(c == n_tiles - 1)
    def _():
        s_out_ref[...] = s_ref[...]


def _gdn_prompt(p3, bat3, conv_w, alog_c, dtb_c, norm, *, qkv_col, z_col):
    b, t, _ = p3.shape
    n_heads = 8
    width = n_heads * HEAD_DIM
    tile = GDN_TILE
    n_tiles = t // tile
    halo_per_tile = tile // SUBLANES
    kern = functools.partial(_gdn_prompt_kernel, n_heads=n_heads, n_tiles=n_tiles)

    def rows(col):
        return pl.BlockSpec((None, tile, width), lambda bi, c: (bi, c, col))

    def halo(col):
        return pl.BlockSpec((None, SUBLANES, width),
                            lambda bi, c: (bi, jnp.maximum(c * halo_per_tile - 1, 0), col))

    def const(shape):
        return pl.BlockSpec(shape, lambda bi, c: tuple(0 for _ in shape))

    return pl.pallas_call(
        kern,
        grid=(b, n_tiles),
        in_specs=[
            rows(qkv_col), rows(qkv_col + 1), rows(qkv_col + 2),
            halo(qkv_col), halo(qkv_col + 1), halo(qkv_col + 2),
            rows(z_col),
            pl.BlockSpec((None, 2 * SUBLANES, tile), lambda bi, c: (bi * n_tiles + c, 0, 0)),
            const((CONV_W, 3 * width)),
            const((n_heads, 1)), const((n_heads, 1)),
            const((1, HEAD_DIM)),
        ],
        out_specs=[
            pl.BlockSpec((None, tile, width), lambda bi, c: (bi, c, 0)),
            pl.BlockSpec((None, n_heads, HEAD_DIM, HEAD_DIM), lambda bi, c: (bi, 0, 0, 0)),
        ],
        out_shape=[
            jax.ShapeDtypeStruct((b, t, width), F32),
            jax.ShapeDtypeStruct((b, n_heads, HEAD_DIM, HEAD_DIM), F32),
        ],
        scratch_shapes=[pltpu.VMEM((n_heads, HEAD_DIM, HEAD_DIM), F32)],
        compiler_params=_params("parallel", "arbitrary"),
        name="gdn_prompt",
    )(p3, p3, p3, p3, p3, p3, p3, bat3, conv_w, alog_c, dtb_c, norm)


def _gdn_sample_kernel(qkv_ref, conv_ref, z_ref, ba_ref, cw_ref, alog_l_ref, dtb_l_ref, norm_ref, s_ref,
                       y_ref, s_out_ref, *, n_heads):
    width3 = qkv_ref.shape[1]
    width = width3 // 3
    rows = qkv_ref.shape[0]
    w = cw_ref[...]
    y = conv_ref[:, 0:width3] * w[0:1, :]
    for i in range(1, CONV_W - 1):
        y = y + conv_ref[:, i * width3:(i + 1) * width3] * w[i:i + 1, :]
    y = y + qkv_ref[...] * w[CONV_W - 1:CONV_W, :]
    c = _silu(y)

    ba = ba_ref[...]
    beta_cols = _sigmoid(ba)
    g_cols = -jnp.exp(alog_l_ref[...]) * _softplus(ba + dtb_l_ref[...])
    decay_cols = jnp.exp(g_cols)
    eye = (lax.broadcasted_iota(jnp.int32, (HEAD_DIM, HEAD_DIM), 0)
           == lax.broadcasted_iota(jnp.int32, (HEAD_DIM, HEAD_DIM), 1)).astype(BF16)

    for h in range(n_heads):
        q = _l2norm(c[:, h * HEAD_DIM:(h + 1) * HEAD_DIM]) * (HEAD_DIM ** -0.5)
        k = _l2norm(c[:, width + h * HEAD_DIM:width + (h + 1) * HEAD_DIM])
        v = c[:, 2 * width + h * HEAD_DIM:2 * width + (h + 1) * HEAD_DIM]
        q_t = _mm_exact_lhs(eye, q, NT_DIMS)
        k_t = _mm_exact_lhs(eye, k, NT_DIMS)
        seqs = range(rows)
        k_col = [k_t[:, r:r + 1] for r in seqs]
        s = [decay_cols[r:r + 1, n_heads + h:n_heads + h + 1] * s_ref[r, h] for r in seqs]
        ks = [jnp.sum(k_col[r] * s[r], axis=0, keepdims=True) for r in seqs]
        u = [beta_cols[r:r + 1, h:h + 1] * (v[r:r + 1, :] - ks[r]) for r in seqs]
        s = [s[r] + k_col[r] * u[r] for r in seqs]
        for r in seqs:
            s_out_ref[r, h] = s[r]
        o = jnp.concatenate([jnp.sum(q_t[:, r:r + 1] * s[r], axis=0, keepdims=True) for r in seqs], axis=0)
        sl = slice(h * HEAD_DIM, (h + 1) * HEAD_DIM)
        y_ref[:, sl] = _gated_rmsnorm(o, z_ref[:, sl], norm_ref[...])


def _gdn_sample(p2, conv_prev2, ba2, states, layer, conv_w, alog_l, dtb_l, norm, *, qkv_col, z_col):
    db = p2.shape[0]
    n_heads = states.shape[2]
    width = n_heads * HEAD_DIM
    rows = SUBLANES
    kern = functools.partial(_gdn_sample_kernel, n_heads=n_heads)

    def const(shape):
        return pl.BlockSpec(shape, lambda i: tuple(0 for _ in shape))

    return pl.pallas_call(
        kern,
        grid=(db // rows,),
        in_specs=[
            pl.BlockSpec((rows, 3 * width), lambda i: (i, qkv_col // 3)),
            pl.BlockSpec((rows, (CONV_W - 1) * 3 * width), lambda i: (i, 0)),
            pl.BlockSpec((rows, width), lambda i: (i, z_col)),
            pl.BlockSpec((rows, HEAD_DIM), lambda i: (i, 0)),
            const((CONV_W, 3 * width)),
            const((1, HEAD_DIM)), const((1, HEAD_DIM)), const((1, HEAD_DIM)),
            pl.BlockSpec((None, rows, n_heads, HEAD_DIM, HEAD_DIM), lambda i: (layer, i, 0, 0, 0)),
        ],
        out_specs=[
            pl.BlockSpec((rows, width), lambda i: (i, 0)),
            pl.BlockSpec((rows, n_heads, HEAD_DIM, HEAD_DIM), lambda i: (i, 0, 0, 0)),
        ],
        out_shape=[
            jax.ShapeDtypeStruct((db, width), F32),
            jax.ShapeDtypeStruct(states.shape[1:], F32),
        ],
        compiler_params=_params("parallel"),
        name="gdn_sample",
    )(p2, conv_prev2, p2, ba2, conv_w, alog_l, dtb_l, norm, states)


def _merge_kernel(x_ref, yg_ref, ym_ref, sg_ref, sm_ref, wpg_ref, wpm_ref, wo_ref, o_ref):
    yg = jnp.dot(yg_ref[...].astype(BF16), wpg_ref[...], preferred_element_type=F32)
    ym = jnp.dot(ym_ref[...].astype(BF16), wpm_ref[...], preferred_element_type=F32)
    mixed = sg_ref[...] * yg + sm_ref[...] * ym
    o_ref[...] = x_ref[...] + jnp.dot(mixed.astype(BF16), wo_ref[...], preferred_element_type=F32)


def _merge(x, yg, ym, p, w_pg, w_pm, w_o, layer, *, tm, sg_col, sm_col):
    m, d = x.shape

    def rows(col):
        return pl.BlockSpec((tm, d), lambda i: (i, col))

    def weight():
        return pl.BlockSpec((None, d, d), lambda i: (layer, 0, 0))

    return pl.pallas_call(
        _merge_kernel,
        grid=(m // tm,),
        in_specs=[rows(0), rows(0), rows(0), rows(sg_col), rows(sm_col), weight(), weight(), weight()],
        out_specs=rows(0),
        out_shape=jax.ShapeDtypeStruct((m, d), F32),
        compiler_params=_params("parallel"),
        name="merge",
    )(x, yg, ym, p, p, w_pg, w_pm, w_o)


def _ffn_kernel(x_ref, ln_ref, wg_ref, wu_ref, wd_ref, lnf_ref, o_ref, xn_ref, acc_ref, *, final):
    j = pl.program_id(1)

    @pl.when(j == 0)
    def _():
        x = x_ref[...]
        y = x * lax.rsqrt(jnp.mean(x * x, axis=-1, keepdims=True) + EPS)
        xn_ref[...] = (y * ln_ref[...]).astype(BF16)
        acc_ref[...] = jnp.zeros_like(acc_ref)

    xn = xn_ref[...]
    g = jnp.dot(xn, wg_ref[...], preferred_element_type=F32)
    u = jnp.dot(xn, wu_ref[...], preferred_element_type=F32)
    acc_ref[...] += jnp.dot((_silu(g) * u).astype(BF16), wd_ref[...], preferred_element_type=F32)

    @pl.when(j == pl.num_programs(1) - 1)
    def _():
        out = x_ref[...] + acc_ref[...]
        if final:
            out = out * lax.rsqrt(jnp.mean(out * out, axis=-1, keepdims=True) + EPS) * lnf_ref[...]
        o_ref[...] = out


def _ffn(x, ln, w_gu, w_dn, ln_final, layer, *, tm, tf, final):
    m, d = x.shape
    d_ff = w_dn.shape[1]
    n_f = d_ff // tf
    return pl.pallas_call(
        functools.partial(_ffn_kernel, final=final),
        grid=(m // tm, n_f),
        in_specs=[
            pl.BlockSpec((tm, d), lambda i, j: (i, 0)),
            pl.BlockSpec((1, d), lambda i, j: (0, 0)),
            pl.BlockSpec((None, d, tf), lambda i, j: (layer, 0, j)),
            pl.BlockSpec((None, d, tf), lambda i, j: (layer, 0, n_f + j)),
            pl.BlockSpec((None, tf, d), lambda i, j: (layer, j, 0)),
            pl.BlockSpec((1, d), lambda i, j: (0, 0)),
        ],
        out_specs=pl.BlockSpec((tm, d), lambda i, j: (i, 0)),
        out_shape=jax.ShapeDtypeStruct((m, d), F32),
        scratch_shapes=[pltpu.VMEM((tm, d), BF16), pltpu.VMEM((tm, d), F32)],
        compiler_params=_params("parallel", "arbitrary"),
        name="ffn",
    )(x, ln, w_gu, w_gu, w_dn, ln_final)


def _rope_tables(pos):
    half = HEAD_DIM // 2
    inv = jnp.exp(-math.log(ROPE_THETA) * jnp.arange(half, dtype=F32) / half)
    ang = pos.astype(F32)[:, None] * inv[None, :]
    cos = jnp.cos(ang)
    sin = jnp.sin(ang)
    return jnp.concatenate([cos, cos], axis=-1), jnp.concatenate([-sin, sin], axis=-1)


def _lane_row(vec, offset):
    return jnp.zeros((1, HEAD_DIM), F32).at[0, offset:offset + vec.shape[0]].set(vec.astype(F32))


def kernel(x_prompt, x_sample, cache_k, cache_v, state_ssm, state_conv, page_table, ln_mix, ln_ffn, w_in,
           conv_w, a_log, dt_bias, gdn_norm, w_branch_gdn, w_branch_moba, w_out, w_gate_up, w_down, ln_final):
    b, t, d = x_prompt.shape
    db, ts, _ = x_sample.shape
    depth = w_in.shape[0]
    n_heads = d // HEAD_DIM
    width = n_heads * HEAD_DIM
    n_pages = page_table.shape[1]
    past = n_pages * PAGE_SIZE
    d_ff = w_down.shape[1]
    assert ts == 1 and n_heads == 8 and width == d
    assert t % MOBA_BLOCK == 0 and t % 1024 == 0 and db % SUBLANES == 0

    c_ba = 7 * width
    c_g = c_ba + 2 * n_heads
    col_q, col_k, col_v, col_qkv, col_z, col_sg, col_sm = 0, 1, 2, 3, 6, 7, 8

    cos_p, sin_p = _rope_tables(jnp.arange(t, dtype=jnp.int32))
    cos_s, sin_s = _rope_tables(jnp.full((db,), past, dtype=jnp.int32))

    tm_p = 1024
    tf = d_ff // 2 if (d_ff // 2) % HEAD_DIM == 0 else d_ff
    xp = x_prompt.reshape(b * t, d)
    xs = x_sample.reshape(db * ts, d)
    outs = {name: [] for name in ("kp", "vp", "sp", "cp", "ks", "vs", "ss", "cs")}
    w_all = w_in.astype(BF16)
    w_gates = w_in[:, :, c_g:].astype(BF16)
    w_ba = jnp.pad(w_in[:, :, c_ba:c_g], ((0, 0), (0, 0), (0, HEAD_DIM - 2 * n_heads))).astype(BF16)
    w_bat = jnp.swapaxes(w_in[:, :, c_ba:c_g], 1, 2).astype(BF16)
    w_pg = w_branch_gdn.astype(BF16)
    w_pm = w_branch_moba.astype(BF16)
    w_o = w_out.astype(BF16)
    w_gu = w_gate_up.astype(BF16)
    w_dn = w_down.astype(BF16)
    for l in range(depth):
        last = l == depth - 1
        ln_m = ln_mix[l].reshape(1, d)
        ln_f = ln_ffn[l].reshape(1, d)
        alog_l = _lane_row(a_log[l], n_heads)
        dtb_l = _lane_row(dt_bias[l], n_heads)
        alog_c = a_log[l].reshape(n_heads, 1).astype(F32)
        dtb_c = dt_bias[l].reshape(n_heads, 1).astype(F32)
        norm = gdn_norm[l].reshape(1, HEAD_DIM)
        proj = functools.partial(_in_proj, tn=width, n_rope_tiles=2, n_lead_tiles=col_sg, k_tile=col_k, v_tile=col_v)

        p, ba, bat, k_new, v_new = proj(xp, ln_m, w_all, w_gates, w_ba, w_bat, cos_p, sin_p, l, tm=tm_p)
        p3 = p.reshape(b, t, -1)
        ym = _moba_prompt(p3, q_col=col_q * n_heads, k_col=col_k * n_heads, v_col=col_v * n_heads)
        bat3 = bat.reshape(2 * n_heads, b * t // GDN_TILE, GDN_TILE).transpose(1, 0, 2)
        yg, s_fin = _gdn_prompt(p3, bat3, conv_w[l], alog_c, dtb_c, norm, qkv_col=col_qkv, z_col=col_z)
        x1 = _merge(xp, yg.reshape(b * t, width), ym.reshape(b * t, width), p, w_pg, w_pm, w_o, l,
                    tm=tm_p // 2, sg_col=col_sg, sm_col=col_sm)
        xp = _ffn(x1, ln_f, w_gu, w_dn, ln_final.reshape(1, d), l, tm=tm_p, tf=tf, final=last)
        outs["kp"].append(k_new)
        outs["vp"].append(v_new)
        outs["sp"].append(s_fin)
        outs["cp"].append(p3[:, t - (CONV_W - 1):, col_qkv * width:(col_qkv + 3) * width])

        p, ba, bat, k_new, v_new = proj(xs, ln_m, w_all, w_gates, w_ba, w_bat, cos_s, sin_s, l, tm=db)
        ph = p.reshape(db, -1, HEAD_DIM)
        ym = _moba_sample(ph, cache_k, cache_v, page_table, l, q_row=col_q, k_row=col_k, v_row=col_v)
        yg, s_new = _gdn_sample(p, state_conv[l].reshape(db, -1), ba, state_ssm, l, conv_w[l], alog_l, dtb_l,
                                norm, qkv_col=col_qkv, z_col=col_z)
        x1 = _merge(xs, yg, ym.reshape(db, width), p, w_pg, w_pm, w_o, l, tm=db, sg_col=col_sg, sm_col=col_sm)
        xs = _ffn(x1, ln_f, w_gu, w_dn, ln_final.reshape(1, d), l, tm=db, tf=tf, final=last)
        qkv_new = p[:, col_qkv * width:(col_qkv + 3) * width]
        outs["ks"].append(k_new)
        outs["vs"].append(v_new)
        outs["ss"].append(s_new)
        outs["cs"].append(jnp.concatenate([state_conv[l][:, 1:], qkv_new[:, None, :]], axis=1))

    n_tpages = t // PAGE_SIZE
    return (
        xp.reshape(b, t, d),
        xs.reshape(db, ts, d),
        jnp.stack(outs["kp"]).reshape(depth, b, n_tpages, PAGE_SIZE, n_heads, HEAD_DIM),
        jnp.stack(outs["vp"]).reshape(depth, b, n_tpages, PAGE_SIZE, n_heads, HEAD_DIM),
        jnp.stack(outs["sp"]),
        jnp.stack(outs["cp"]),
        jnp.stack(outs["ks"]).reshape(depth, db, ts, n_heads, HEAD_DIM),
        jnp.stack(outs["vs"]).reshape(depth, db, ts, n_heads, HEAD_DIM),
        jnp.stack(outs["ss"]),
        jnp.stack(outs["cs"]),
    )
```
